```python
import jax, jax.numpy as jnp
from jax import lax
import numpy as np

D_MODEL = 1024
BATCH = 8
SEQ = 2048
DEPTH = 4

N_MIXERS = 2
EXPAND = 2
D_INNER = EXPAND * D_MODEL
CHUNK = 128
A_GROUPS = 8
CONV_W = 3
EPS = 1e-6
N_A = (DEPTH + 1) // 2
N_B = DEPTH // 2

kernel_name = "hybrid_gmlp_shortconv_trunk"


def rmsnorm(x, g):
    xf = x.astype(jnp.float32)
    y = xf * lax.rsqrt(jnp.mean(xf * xf, axis=-1, keepdims=True) + EPS)
    return (y * g.astype(jnp.float32)).astype(x.dtype)


def mixer_a(h, w_in, v_norm_g, w_s, b_s, w_out):
    b, s, _ = h.shape
    u, v, z = jnp.split(h @ w_in, 3, axis=-1)
    v = rmsnorm(v, v_norm_g)
    vc = v.reshape(b, s // CHUNK, CHUNK, A_GROUPS, D_INNER // A_GROUPS)
    causal = jnp.tril(jnp.ones((CHUNK, CHUNK), dtype=bool))
    ws = jnp.where(causal[None], w_s, jnp.zeros((), w_s.dtype))
    mixed = jnp.einsum("gts,bnsgc->bntgc", ws, vc)
    mixed = mixed + jnp.transpose(b_s)[None, None, :, :, None]
    mixed = mixed.reshape(b, s, D_INNER)
    y = u * mixed * jax.nn.silu(z)
    return y @ w_out


def mixer_b(h, w_in, w_conv, w_out):
    s = h.shape[1]
    bg, cg, xs, z = jnp.split(h @ w_in, 4, axis=-1)
    xc = cg * xs
    xp = jnp.pad(xc, ((0, 0), (CONV_W - 1, 0), (0, 0)))
    conv = w_conv[0] * xp[:, 0:s, :]
    for k in range(1, CONV_W):
        conv = conv + w_conv[k] * xp[:, k:k + s, :]
    y = bg * conv * jax.nn.silu(z)
    return y @ w_out


def setup_inputs(seed: int = 0) -> dict:
    key = jax.random.key(seed)
    ks = jax.random.split(key, 12)
    f32 = jnp.float32
    x = jax.random.normal(ks[0], (BATCH, SEQ, D_MODEL), f32)
    norm_g = 1.0 + 0.02 * jax.random.normal(ks[1], (DEPTH, D_MODEL), f32)
    final_g = 1.0 + 0.02 * jax.random.normal(ks[2], (D_MODEL,), f32)
    a_w_in = jax.random.normal(ks[3], (N_A, D_MODEL, 3 * D_INNER), f32) * D_MODEL ** -0.5
    a_v_norm_g = 1.0 + 0.02 * jax.random.normal(ks[4], (N_A, D_INNER), f32)
    a_w_s = jax.random.normal(ks[5], (N_A, A_GROUPS, CHUNK, CHUNK), f32) * CHUNK ** -0.5
    a_b_s = 1.0 + 0.1 * jax.random.normal(ks[6], (N_A, A_GROUPS, CHUNK), f32)
    a_w_out = jax.random.normal(ks[7], (N_A, D_INNER, D_MODEL), f32) * D_INNER ** -0.5
    b_w_in = jax.random.normal(ks[8], (N_B, D_MODEL, 4 * D_INNER), f32) * D_MODEL ** -0.5
    b_w_conv = jax.random.normal(ks[9], (N_B, CONV_W, D_INNER), f32) * CONV_W ** -0.5
    b_w_out = jax.random.normal(ks[10], (N_B, D_INNER, D_MODEL), f32) * D_INNER ** -0.5
    return {"x": x, "norm_g": norm_g, "final_g": final_g,
            "a_w_in": a_w_in, "a_v_norm_g": a_v_norm_g, "a_w_s": a_w_s, "a_b_s": a_b_s,
            "a_w_out": a_w_out, "b_w_in": b_w_in, "b_w_conv": b_w_conv, "b_w_out": b_w_out}


def reference(x, norm_g, final_g, a_w_in, a_v_norm_g, a_w_s, a_b_s, a_w_out,
              b_w_in, b_w_conv, b_w_out):
    for i in range(DEPTH):
        h = rmsnorm(x, norm_g[i])
        j = i // N_MIXERS
        if i % N_MIXERS == 0:
            x = x + mixer_a(h, a_w_in[j], a_v_norm_g[j], a_w_s[j], a_b_s[j], a_w_out[j])
        else:
            x = x + mixer_b(h, b_w_in[j], b_w_conv[j], b_w_out[j])
    return rmsnorm(x, final_g)
```

```python
import functools

import jax
import jax.numpy as jnp
from jax import lax
from jax.experimental import pallas as pl
from jax.experimental.pallas import tpu as pltpu

EPS = 1e-6
CHUNK = 128
CONV_W = 3
SUBLANES = 8
VMEM_LIMIT_BYTES = 60 * 1024 * 1024

F32 = jnp.float32
BF16 = jnp.bfloat16


def _rms_scale(x):
    return lax.rsqrt(jnp.mean(x * x, axis=-1, keepdims=True) + EPS)


def _silu(z):
    return z * jax.nn.sigmoid(z)


def _layer_a_kernel(x_ref, g_ref, win_ref, vg_ref, ws_ref, bias_ref, wout_ref,
                    o_ref, h_ref, vn_ref, y_ref, *, tm, d_inner, groups):
    x = x_ref[...]
    h_ref[...] = (x * _rms_scale(x) * g_ref[...]).astype(BF16)
    h = h_ref[...]
    v = jnp.dot(h, win_ref[:, d_inner:2 * d_inner], preferred_element_type=F32)
    vn_ref[...] = (v * _rms_scale(v) * vg_ref[...]).astype(BF16)
    gw = d_inner // groups
    for g in range(groups):
        lo = g * gw
        u = jnp.dot(h, win_ref[:, lo:lo + gw], preferred_element_type=F32)
        z = jnp.dot(h, win_ref[:, 2 * d_inner + lo:2 * d_inner + lo + gw],
                    preferred_element_type=F32)
        ws = ws_ref[g]
        for n in range(tm // CHUNK):
            r0 = n * CHUNK
            mixed = jnp.dot(ws, vn_ref[r0:r0 + CHUNK, lo:lo + gw],
                            preferred_element_type=F32) + bias_ref[:, lo:lo + gw]
            y = u[r0:r0 + CHUNK] * mixed * _silu(z[r0:r0 + CHUNK])
            y_ref[r0:r0 + CHUNK, lo:lo + gw] = y.astype(BF16)
    o_ref[...] = x + jnp.dot(y_ref[...], wout_ref[...], preferred_element_type=F32)


def _layer_b_kernel(x_ref, g_ref, win_ref, wc_ref, wout_ref,
                    o_ref, h_ref, xc_ref, carry_ref, y_ref, *,
                    tm, d_inner, col_block, tiles_per_seq, final_norm, fg_ref=None):
    @pl.when(pl.program_id(0) % tiles_per_seq == 0)
    def _():
        carry_ref[...] = jnp.zeros_like(carry_ref)

    x = x_ref[...]
    h_ref[...] = (x * _rms_scale(x) * g_ref[...]).astype(BF16)
    h = h_ref[...]
    for j in range(d_inner // col_block):
        lo = j * col_block

        def proj(k, lo=lo):
            return jnp.dot(h, win_ref[:, k * d_inner + lo:k * d_inner + lo + col_block],
                           preferred_element_type=F32)

        xc = proj(1) * proj(2)
        xc_ref[0:SUBLANES, :] = carry_ref[:, lo:lo + col_block]
        xc_ref[SUBLANES:SUBLANES + tm, :] = xc
        carry_ref[:, lo:lo + col_block] = xc[tm - SUBLANES:tm]
        conv = wc_ref[CONV_W - 1:CONV_W, lo:lo + col_block] * xc
        for k in range(CONV_W - 1):
            shift = CONV_W - 1 - k
            conv = conv + (wc_ref[k:k + 1, lo:lo + col_block]
                           * xc_ref[SUBLANES - shift:SUBLANES - shift + tm, :])
        y = proj(0) * conv * _silu(proj(3))
        y_ref[:, lo:lo + col_block] = y.astype(BF16)
    out = x + jnp.dot(y_ref[...], wout_ref[...], preferred_element_type=F32)
    if final_norm:
        out = out * _rms_scale(out) * fg_ref[...]
    o_ref[...] = out


def _layer_b_final_kernel(x_ref, g_ref, win_ref, wc_ref, wout_ref, fg_ref, o_ref, *scratch, **kw):
    _layer_b_kernel(x_ref, g_ref, win_ref, wc_ref, wout_ref, o_ref, *scratch, fg_ref=fg_ref, **kw)


def _resident(shape):
    return pl.BlockSpec(shape, lambda i: (0,) * len(shape), pipeline_mode=pl.Buffered(1))


def _compiler_params():
    return pltpu.CompilerParams(dimension_semantics=("arbitrary",),
                                vmem_limit_bytes=VMEM_LIMIT_BYTES)


def _layer_a(x2, g, w_in, v_g, w_s, b_s, w_out, *, tm):
    n_tok, d_model = x2.shape
    d_inner = w_out.shape[0]
    groups = w_s.shape[0]
    gw = d_inner // groups
    causal = jnp.tril(jnp.ones((CHUNK, CHUNK), dtype=bool))
    ws = jnp.where(causal[None], w_s, jnp.zeros((), w_s.dtype)).astype(BF16)
    bias = jnp.repeat(jnp.transpose(b_s), gw, axis=1)
    row_tile = pl.BlockSpec((tm, d_model), lambda i: (i, 0))
    kern = functools.partial(_layer_a_kernel, tm=tm, d_inner=d_inner, groups=groups)
    return pl.pallas_call(
        kern,
        grid=(n_tok // tm,),
        in_specs=[row_tile,
                  _resident((1, d_model)),
                  _resident(w_in.shape),
                  _resident((1, d_inner)),
                  _resident(ws.shape),
                  _resident(bias.shape),
                  _resident(w_out.shape)],
        out_specs=row_tile,
        out_shape=jax.ShapeDtypeStruct(x2.shape, x2.dtype),
        scratch_shapes=[pltpu.VMEM((tm, d_model), BF16),
                        pltpu.VMEM((tm, d_inner), BF16),
                        pltpu.VMEM((tm, d_inner), BF16)],
        compiler_params=_compiler_params(),
        name="gmlp_layer",
    )(x2, g.reshape(1, d_model), w_in.astype(BF16), v_g.reshape(1, d_inner), ws, bias,
      w_out.astype(BF16))


def _layer_b(x2, g, w_in, w_conv, w_out, final_g, *, tm, seq, col_block):
    n_tok, d_model = x2.shape
    d_inner = w_out.shape[0]
    row_tile = pl.BlockSpec((tm, d_model), lambda i: (i, 0))
    kw = dict(tm=tm, d_inner=d_inner, col_block=col_block, tiles_per_seq=seq // tm,
              final_norm=final_g is not None)
    in_specs = [row_tile,
                _resident((1, d_model)),
                _resident(w_in.shape),
                _resident(w_conv.shape),
                _resident(w_out.shape)]
    args = [x2, g.reshape(1, d_model), w_in.astype(BF16), w_conv, w_out.astype(BF16)]
    if final_g is None:
        kern = functools.partial(_layer_b_kernel, **kw)
    else:
        kern = functools.partial(_layer_b_final_kernel, **kw)
        in_specs.append(_resident((1, d_model)))
        args.append(final_g.reshape(1, d_model))
    return pl.pallas_call(
        kern,
        grid=(n_tok // tm,),
        in_specs=in_specs,
        out_specs=row_tile,
        out_shape=jax.ShapeDtypeStruct(x2.shape, x2.dtype),
        scratch_shapes=[pltpu.VMEM((tm, d_model), BF16),
                        pltpu.VMEM((tm + SUBLANES, col_block), F32),
                        pltpu.VMEM((SUBLANES, d_inner), F32),
                        pltpu.VMEM((tm, d_inner), BF16)],
        compiler_params=_compiler_params(),
        name="shortconv_layer",
    )(*args)


def kernel(x, norm_g, final_g, a_w_in, a_v_norm_g, a_w_s, a_b_s, a_w_out, b_w_in, b_w_conv, b_w_out):
    batch, seq, d_model = x.shape
    depth = norm_g.shape[0]
    tm = 512
    assert seq % tm == 0 and tm % CHUNK == 0
    x2 = x.reshape(batch * seq, d_model)
    for i in range(depth):
        j = i // 2
        if i % 2 == 0:
            x2 = _layer_a(x2, norm_g[i], a_w_in[j], a_v_norm_g[j], a_w_s[j], a_b_s[j], a_w_out[j],
                          tm=tm)
        else:
            x2 = _layer_b(x2, norm_g[i], b_w_in[j], b_w_conv[j], b_w_out[j],
                          final_g if i == depth - 1 else None, tm=tm, seq=seq, col_block=512)
    if depth % 2 == 1:
        raise NotImplementedError("final norm is fused into the last short-conv layer")
    return x2.reshape(batch, seq, d_model)
```

```python
import functools

import jax
import jax.numpy as jnp
from jax import lax
from jax.experimental import pallas as pl
from jax.experimental.pallas import tpu as pltpu

EPS = 1e-6
CHUNK = 128
CONV_W = 3
SUBLANES = 8
BF16_ROWS = 16
VMEM_LIMIT_BYTES = 60 * 1024 * 1024

F32 = jnp.float32
BF16 = jnp.bfloat16


def _rms_scale(x):
    return lax.rsqrt(jnp.mean(x * x, axis=-1, keepdims=True) + EPS)


def _silu(z):
    return z * jax.nn.sigmoid(z)


def _round_next_weights(refs, n_next):
    for src, dst in zip(refs[:n_next], refs[n_next:2 * n_next]):
        dst[...] = src[...].astype(BF16)


def _layer_a_kernel(x_ref, g_ref, win_ref, vg_ref, ws_ref, bias_ref, wout_ref, *rest,
                    tm, d_inner, groups, n_next):
    o_ref = rest[n_next]
    h_ref, vn_ref, y_ref = rest[2 * n_next + 1:]
    _round_next_weights(rest[:n_next] + rest[n_next + 1:2 * n_next + 1], n_next)

    x = x_ref[...]
    h_ref[...] = (x * _rms_scale(x) * g_ref[...]).astype(BF16)
    h = h_ref[...]
    v = jnp.dot(h, win_ref[:, d_inner:2 * d_inner], preferred_element_type=F32)
    vn_ref[...] = (v * _rms_scale(v) * vg_ref[...]).astype(BF16)
    gw = d_inner // groups
    for g in range(groups):
        lo = g * gw
        u = jnp.dot(h, win_ref[:, lo:lo + gw], preferred_element_type=F32)
        z = jnp.dot(h, win_ref[:, 2 * d_inner + lo:2 * d_inner + lo + gw],
                    preferred_element_type=F32)
        ws = ws_ref[g]
        for n in range(tm // CHUNK):
            r0 = n * CHUNK
            mixed = jnp.dot(ws, vn_ref[r0:r0 + CHUNK, lo:lo + gw],
                            preferred_element_type=F32) + bias_ref[:, lo:lo + gw]
            y = u[r0:r0 + CHUNK] * mixed * _silu(z[r0:r0 + CHUNK])
            y_ref[r0:r0 + CHUNK, lo:lo + gw] = y.astype(BF16)
    o_ref[...] = x + jnp.dot(y_ref[...], wout_ref[...], preferred_element_type=F32)


def _layer_b_kernel(x_ref, g_ref, win_ref, wc_ref, wout_ref, fg_ref, *rest,
                    tm, d_inner, col_block, tiles_per_seq, final_norm, n_next):
    o_ref = rest[n_next]
    h_ref, xc_ref, carry_ref, y_ref = rest[2 * n_next + 1:]
    _round_next_weights(rest[:n_next] + rest[n_next + 1:2 * n_next + 1], n_next)

    @pl.when(pl.program_id(0) % tiles_per_seq == 0)
    def _():
        carry_ref[...] = jnp.zeros_like(carry_ref)

    x = x_ref[...]
    h_ref[...] = (x * _rms_scale(x) * g_ref[...]).astype(BF16)
    h = h_ref[...]
    for j in range(d_inner // col_block):
        lo = j * col_block

        def proj(k, lo=lo):
            return jnp.dot(h, win_ref[:, k * d_inner + lo:k * d_inner + lo + col_block],
                           preferred_element_type=F32)

        xc = proj(1) * proj(2)
        xc_ref[0:SUBLANES, :] = carry_ref[:, lo:lo + col_block]
        xc_ref[SUBLANES:SUBLANES + tm, :] = xc
        carry_ref[:, lo:lo + col_block] = xc[tm - SUBLANES:tm]
        conv = wc_ref[CONV_W - 1:CONV_W, lo:lo + col_block] * xc
        for k in range(CONV_W - 1):
            shift = CONV_W - 1 - k
            conv = conv + (wc_ref[k:k + 1, lo:lo + col_block]
                           * xc_ref[SUBLANES - shift:SUBLANES - shift + tm, :])
        y = proj(0) * conv * _silu(proj(3))
        y_ref[:, lo:lo + col_block] = y.astype(BF16)
    out = x + jnp.dot(y_ref[...], wout_ref[...], preferred_element_type=F32)
    if final_norm:
        out = out * _rms_scale(out) * fg_ref[...]
    o_ref[...] = out


def _resident(shape):
    return pl.BlockSpec(shape, lambda i: (0,) * len(shape), pipeline_mode=pl.Buffered(1))


def _slab_rows(w, n_steps):
    rows = w.shape[-2] // n_steps
    assert rows * n_steps == w.shape[-2] and rows % BF16_ROWS == 0
    return rows


def _call_layer(kern, x2, resident_args, next_weights, scratch_shapes, *, tm, name):
    n_tok, d_model = x2.shape
    n_steps = n_tok // tm
    row_tile = pl.BlockSpec((tm, d_model), lambda i: (i, 0))
    slabs_in = [pl.BlockSpec((None, _slab_rows(w, n_steps), w.shape[2]),
                             lambda i, layer=layer: (layer, i, 0)) for w, layer in next_weights]
    slabs_out = [pl.BlockSpec((_slab_rows(w, n_steps), w.shape[2]), lambda i: (i, 0))
                 for w, _ in next_weights]
    outs = pl.pallas_call(
        kern,
        grid=(n_steps,),
        in_specs=[row_tile] + [_resident(a.shape) for a in resident_args] + slabs_in,
        out_specs=[row_tile] + slabs_out,
        out_shape=[jax.ShapeDtypeStruct(x2.shape, x2.dtype)]
                  + [jax.ShapeDtypeStruct(w.shape[1:], BF16) for w, _ in next_weights],
        scratch_shapes=scratch_shapes,
        compiler_params=pltpu.CompilerParams(dimension_semantics=("arbitrary",),
                                             vmem_limit_bytes=VMEM_LIMIT_BYTES),
        name=name,
    )(x2, *resident_args, *[w for w, _ in next_weights])
    return outs[0], outs[1:]


def _layer_a(x2, g, w_in, v_g, w_s, b_s, w_out, next_weights, *, tm):
    d_model = x2.shape[1]
    d_inner = w_out.shape[0]
    groups = w_s.shape[0]
    gw = d_inner // groups
    causal = jnp.tril(jnp.ones((CHUNK, CHUNK), dtype=bool))
    ws = jnp.where(causal[None], w_s, jnp.zeros((), w_s.dtype)).astype(BF16)
    bias = jnp.repeat(jnp.transpose(b_s), gw, axis=1)
    kern = functools.partial(_layer_a_kernel, tm=tm, d_inner=d_inner, groups=groups,
                             n_next=len(next_weights))
    return _call_layer(
        kern, x2,
        [g.reshape(1, d_model), w_in, v_g.reshape(1, d_inner), ws, bias, w_out],
        next_weights,
        [pltpu.VMEM((tm, d_model), BF16),
         pltpu.VMEM((tm, d_inner), BF16),
         pltpu.VMEM((tm, d_inner), BF16)],
        tm=tm, name="gmlp_layer")


def _layer_b(x2, g, w_in, w_conv, w_out, final_g, next_weights, *, tm, seq, col_block, final_norm):
    d_model = x2.shape[1]
    d_inner = w_out.shape[0]
    kern = functools.partial(_layer_b_kernel, tm=tm, d_inner=d_inner, col_block=col_block,
                             tiles_per_seq=seq // tm, final_norm=final_norm,
                             n_next=len(next_weights))
    return _call_layer(
        kern, x2,
        [g.reshape(1, d_model), w_in, w_conv, w_out, final_g.reshape(1, d_model)],
        next_weights,
        [pltpu.VMEM((tm, d_model), BF16),
         pltpu.VMEM((tm + SUBLANES, col_block), F32),
         pltpu.VMEM((SUBLANES, d_inner), F32),
         pltpu.VMEM((tm, d_inner), BF16)],
        tm=tm, name="shortconv_layer")


def kernel(x, norm_g, final_g, a_w_in, a_v_norm_g, a_w_s, a_b_s, a_w_out, b_w_in, b_w_conv, b_w_out):
    batch, seq, d_model = x.shape
    depth = norm_g.shape[0]
    tm = 512
    assert seq % tm == 0 and tm % CHUNK == 0
    assert depth % 2 == 0, "the final norm is fused into the last short-conv layer"
    x2 = x.reshape(batch * seq, d_model)
    w_in, w_out = a_w_in[0].astype(BF16), a_w_out[0].astype(BF16)
    for i in range(depth):
        j = i // 2
        if i % 2 == 0:
            x2, (w_in, w_out) = _layer_a(
                x2, norm_g[i], w_in, a_v_norm_g[j], a_w_s[j], a_b_s[j], w_out,
                [(b_w_in, j), (b_w_out, j)], tm=tm)
        else:
            last = i == depth - 1
            x2, nxt = _layer_b(
                x2, norm_g[i], w_in, b_w_conv[j], w_out, final_g,
                [] if last else [(a_w_in, j + 1), (a_w_out, j + 1)],
                tm=tm, seq=seq, col_block=512, final_norm=last)
            if not last:
                w_in, w_out = nxt
    return x2.reshape(batch, seq, d_model)
```

```python
import functools

import jax
import jax.numpy as jnp
from jax import lax
from jax.experimental import pallas as pl
from jax.experimental.pallas import tpu as pltpu

EPS = 1e-6
CHUNK = 128
CONV_W = 3
SUBLANES = 8
BF16_ROWS = 16
VMEM_LIMIT_BYTES = 60 * 1024 * 1024
STAGE_SHAPE = (6, 256, 1024)

F32 = jnp.float32
BF16 = jnp.bfloat16


def _rms_scale(x):
    return lax.rsqrt(jnp.mean(x * x, axis=-1, keepdims=True) + EPS)


def _silu(z):
    return z * jax.nn.sigmoid(z)


def _round_next_weights(srcs, dsts):
    for src, dst in zip(srcs, dsts):
        dst[...] = src[...].astype(BF16)


def _fetch_and_round(w_hbm, layer, dst_ref, stage_ref, sem):
    max_slots, rows, cols = stage_ref.shape
    k, n = dst_ref.shape
    per_band = n // cols
    n_panels = (k // rows) * per_band
    assert n == per_band * cols and k % rows == 0
    n_slots = max(d for d in range(per_band, max_slots + 1, per_band) if n_panels % d == 0)
    bands_per_group = n_slots // per_band
    n_groups = n_panels // n_slots

    def copy(g, s):
        band, col = divmod(s, per_band)
        r0 = (g * bands_per_group + band) * rows
        c0 = col * cols
        dma = pltpu.make_async_copy(w_hbm.at[layer, pl.ds(r0, rows), pl.ds(c0, cols)],
                                    stage_ref.at[s], sem.at[s])
        return dma, r0, c0

    for s in range(n_slots):
        copy(0, s)[0].start()

    def body(g, carry):
        for s in range(n_slots):
            dma, r0, c0 = copy(g, s)
            dma.wait()
            dst_ref[pl.ds(pl.multiple_of(r0, rows), rows), c0:c0 + cols] = (
                stage_ref[s].astype(BF16))

            @pl.when(g + 1 < n_groups)
            def _():
                copy(g + 1, s)[0].start()
        return carry

    lax.fori_loop(0, n_groups, body, 0)


def _split_refs(rest, n_next, n_scratch):
    assert len(rest) == 2 * n_next + 1 + n_scratch
    return rest[:n_next], rest[n_next], rest[n_next + 1:2 * n_next + 1], rest[2 * n_next + 1:]


def _layer_a_kernel(x_ref, g_ref, vg_ref, ws_ref, bias_ref, win_in, wout_in, *rest,
                    tm, sub_m, d_inner, groups, layer, sub, n_next, own_weights):
    nxt_src, o_ref, nxt_dst, scratch = _split_refs(rest, n_next, 8 if own_weights else 4)
    h_ref, v_ref, vn_ref, y_ref = scratch[:4]
    if own_weights:
        win_ref, wout_ref, stage_ref, sem = scratch[4:]

        @pl.when(pl.program_id(0) == 0)
        def _():
            _fetch_and_round(win_in, sub, win_ref, stage_ref, sem)
            _fetch_and_round(wout_in, sub, wout_ref, stage_ref, sem)
    else:
        win_ref, wout_ref = win_in, wout_in
    _round_next_weights(nxt_src, nxt_dst)

    gw = d_inner // groups
    for s0 in range(0, tm, sub_m):
        rows = slice(s0, s0 + sub_m)
        x = x_ref[rows, :]
        h_ref[rows, :] = (x * _rms_scale(x) * g_ref[layer:layer + 1, :]).astype(BF16)
        h = h_ref[rows, :]
        ssq = jnp.zeros((sub_m, 1), F32)
        for g in range(groups):
            lo = g * gw
            v = jnp.dot(h, win_ref[:, d_inner + lo:d_inner + lo + gw], preferred_element_type=F32)
            v_ref[rows, lo:lo + gw] = v
            ssq = ssq + jnp.sum(v * v, axis=-1, keepdims=True)
        v_scale = lax.rsqrt(ssq / d_inner + EPS)
        for g in range(groups):
            lo = g * gw
            vn_ref[rows, lo:lo + gw] = (v_ref[rows, lo:lo + gw] * v_scale
                                        * vg_ref[sub:sub + 1, lo:lo + gw]).astype(BF16)
            u = jnp.dot(h, win_ref[:, lo:lo + gw], preferred_element_type=F32)
            z = jnp.dot(h, win_ref[:, 2 * d_inner + lo:2 * d_inner + lo + gw],
                        preferred_element_type=F32)
            ws = ws_ref[g]
            for n in range(sub_m // CHUNK):
                r0 = n * CHUNK
                mixed = jnp.dot(ws, vn_ref[s0 + r0:s0 + r0 + CHUNK, lo:lo + gw],
                                preferred_element_type=F32) + bias_ref[:, lo:lo + gw]
                y = u[r0:r0 + CHUNK] * mixed * _silu(z[r0:r0 + CHUNK])
                y_ref[s0 + r0:s0 + r0 + CHUNK, lo:lo + gw] = y.astype(BF16)
        o_ref[rows, :] = x + jnp.dot(y_ref[rows, :], wout_ref[...], preferred_element_type=F32)


def _layer_b_kernel(x_ref, g_ref, wc_ref, fg_ref, win_ref, wout_ref, *rest,
                    tm, sub_m, d_inner, col_block, tiles_per_seq, layer, final_norm, n_next):
    nxt_src, o_ref, nxt_dst, (h_ref, xc_ref, carry_ref, y_ref) = _split_refs(rest, n_next, 4)
    _round_next_weights(nxt_src, nxt_dst)

    @pl.when(pl.program_id(0) % tiles_per_seq == 0)
    def _():
        carry_ref[...] = jnp.zeros_like(carry_ref)

    for s, s0 in enumerate(range(0, tm, sub_m)):
        rows = slice(s0, s0 + sub_m)
        x = x_ref[rows, :]
        h_ref[rows, :] = (x * _rms_scale(x) * g_ref[layer:layer + 1, :]).astype(BF16)
        h = h_ref[rows, :]
        for j in range(d_inner // col_block):
            cols = slice(j * col_block, (j + 1) * col_block)

            def proj(k, lo=j * col_block, h=h):
                return jnp.dot(h, win_ref[:, k * d_inner + lo:k * d_inner + lo + col_block],
                               preferred_element_type=F32)

            xc = proj(1) * proj(2)
            xc_ref[s, 0:SUBLANES, :] = carry_ref[:, cols]
            xc_ref[s, SUBLANES:SUBLANES + sub_m, :] = xc
            carry_ref[:, cols] = xc[sub_m - SUBLANES:sub_m]
            conv = wc_ref[CONV_W - 1:CONV_W, cols] * xc
            for k in range(CONV_W - 1):
                shift = CONV_W - 1 - k
                conv = conv + (wc_ref[k:k + 1, cols]
                               * xc_ref[s, SUBLANES - shift:SUBLANES - shift + sub_m, :])
            y = proj(0) * conv * _silu(proj(3))
            y_ref[rows, cols] = y.astype(BF16)
        out = x + jnp.dot(y_ref[rows, :], wout_ref[...], preferred_element_type=F32)
        if final_norm:
            out = out * _rms_scale(out) * fg_ref[...]
        o_ref[rows, :] = out


def _resident(a, layer=None):
    if layer is None:
        return pl.BlockSpec(a.shape, lambda i: (0,) * a.ndim, pipeline_mode=pl.Buffered(1))
    return pl.BlockSpec((None,) + a.shape[1:], lambda i: (layer,) + (0,) * (a.ndim - 1),
                        pipeline_mode=pl.Buffered(1))


def _slab_rows(w, n_steps):
    rows = w.shape[-2] // n_steps
    assert rows * n_steps == w.shape[-2] and rows % BF16_ROWS == 0
    return rows


def _call_layer(kern, x2, args, in_specs, next_weights, scratch_shapes, *, tm, name):
    n_tok, d_model = x2.shape
    n_steps = n_tok // tm
    row_tile = pl.BlockSpec((tm, d_model), lambda i: (i, 0))
    slabs_in = [pl.BlockSpec((None, _slab_rows(w, n_steps), w.shape[2]),
                             lambda i, layer=layer: (layer, i, 0)) for w, layer in next_weights]
    slabs_out = [pl.BlockSpec((_slab_rows(w, n_steps), w.shape[2]), lambda i: (i, 0))
                 for w, _ in next_weights]
    outs = pl.pallas_call(
        kern,
        grid=(n_steps,),
        in_specs=[row_tile] + in_specs + slabs_in,
        out_specs=[row_tile] + slabs_out,
        out_shape=[jax.ShapeDtypeStruct(x2.shape, x2.dtype)]
                  + [jax.ShapeDtypeStruct(w.shape[1:], BF16) for w, _ in next_weights],
        scratch_shapes=scratch_shapes,
        compiler_params=pltpu.CompilerParams(dimension_semantics=("arbitrary",),
                                             vmem_limit_bytes=VMEM_LIMIT_BYTES),
        name=name,
    )(x2, *args, *[w for w, _ in next_weights])
    return outs[0], outs[1:]


def _layer_a(x2, norm_g, v_g, ws, bias, w_in, w_out, next_weights, *,
             tm, sub_m, layer, sub, own_weights):
    d_model = x2.shape[1]
    d_inner = w_out.shape[-2]
    kern = functools.partial(_layer_a_kernel, tm=tm, sub_m=sub_m, d_inner=d_inner,
                             groups=ws.shape[1],
                             layer=layer, sub=sub, n_next=len(next_weights),
                             own_weights=own_weights)
    scratch = [pltpu.VMEM((tm, d_model), BF16),
               pltpu.VMEM((tm, d_inner), F32),
               pltpu.VMEM((tm, d_inner), BF16),
               pltpu.VMEM((tm, d_inner), BF16)]
    if own_weights:
        w_specs = [pl.BlockSpec(memory_space=pl.ANY)] * 2
        scratch += [pltpu.VMEM(w_in.shape[1:], BF16),
                    pltpu.VMEM(w_out.shape[1:], BF16),
                    pltpu.VMEM(STAGE_SHAPE, F32),
                    pltpu.SemaphoreType.DMA(STAGE_SHAPE[:1])]
    else:
        w_specs = [_resident(w_in), _resident(w_out)]
    return _call_layer(
        kern, x2,
        [norm_g, v_g, ws, bias, w_in, w_out],
        [_resident(norm_g), _resident(v_g), _resident(ws, sub), _resident(bias, sub)] + w_specs,
        next_weights, scratch, tm=tm, name="gmlp_layer")


def _layer_b(x2, norm_g, w_conv, final_g, w_in, w_out, next_weights, *,
             tm, sub_m, seq, col_block, layer, sub, final_norm):
    d_model = x2.shape[1]
    d_inner = w_out.shape[0]
    kern = functools.partial(_layer_b_kernel, tm=tm, sub_m=sub_m, d_inner=d_inner,
                             col_block=col_block,
                             tiles_per_seq=seq // tm, layer=layer, final_norm=final_norm,
                             n_next=len(next_weights))
    return _call_layer(
        kern, x2,
        [norm_g, w_conv, final_g, w_in, w_out],
        [_resident(norm_g), _resident(w_conv, sub), _resident(final_g),
         _resident(w_in), _resident(w_out)],
        next_weights,
        [pltpu.VMEM((tm, d_model), BF16),
         pltpu.VMEM((tm // sub_m, sub_m + SUBLANES, col_block), F32),
         pltpu.VMEM((SUBLANES, d_inner), F32),
         pltpu.VMEM((tm, d_inner), BF16)],
        tm=tm, name="shortconv_layer")


def kernel(x, norm_g, final_g, a_w_in, a_v_norm_g, a_w_s, a_b_s, a_w_out, b_w_in, b_w_conv, b_w_out):
    batch, seq, d_model = x.shape
    depth = norm_g.shape[0]
    d_inner = a_w_out.shape[1]
    tm, sub_m = 1024, 1024
    assert seq % tm == 0 and tm % sub_m == 0 and sub_m % CHUNK == 0
    assert depth % 2 == 0, "the final norm is fused into the last short-conv layer"
    x2 = x.reshape(batch * seq, d_model)
    causal = jnp.tril(jnp.ones((CHUNK, CHUNK), dtype=bool))
    ws = jnp.where(causal, a_w_s, jnp.zeros((), a_w_s.dtype)).astype(BF16)
    bias = jnp.repeat(jnp.swapaxes(a_b_s, 1, 2), d_inner // a_w_s.shape[1], axis=2)
    final_g = final_g.reshape(1, d_model)
    w_in, w_out = a_w_in, a_w_out
    for i in range(depth):
        j = i // 2
        if i % 2 == 0:
            x2, (w_in, w_out) = _layer_a(
                x2, norm_g, a_v_norm_g, ws, bias, w_in, w_out,
                [(b_w_in, j), (b_w_out, j)], tm=tm, sub_m=sub_m, layer=i, sub=j,
                own_weights=i == 0)
        else:
            last = i == depth - 1
            x2, nxt = _layer_b(
                x2, norm_g, b_w_conv, final_g, w_in, w_out,
                [] if last else [(a_w_in, j + 1), (a_w_out, j + 1)],
                tm=tm, sub_m=sub_m, seq=seq, col_block=256, layer=i, sub=j, final_norm=last)
            if not last:
                w_in, w_out = nxt
    return x2.reshape(batch, seq, d_model)
```

```python
import functools

import jax
import jax.numpy as jnp
from jax import lax
from jax.experimental import pallas as pl
from jax.experimental.pallas import tpu as pltpu

EPS = 1e-6
CHUNK = 128
CONV_W = 3
SUBLANES = 8
BF16_ROWS = 16
VMEM_LIMIT_BYTES = 60 * 1024 * 1024
STAGE_SHAPE = (6, 256, 1024)

F32 = jnp.float32
BF16 = jnp.bfloat16


def _mean_sq(x):
    return jnp.mean(x * x, axis=-1, keepdims=True)


def _rms_scale(x):
    return lax.rsqrt(_mean_sq(x) + EPS)


def _silu(z):
    return z * jax.nn.sigmoid(z)


def _round_next_weights(srcs, gain_ref, gain_col, dsts):
    if srcs:
        w_in, w_out = srcs
        dsts[0][...] = (w_in[...] * gain_ref[:, gain_col:gain_col + 1]).astype(BF16)
        dsts[1][...] = w_out[...].astype(BF16)


def _fetch_and_round(w_hbm, layer, dst_ref, stage_ref, sem, row_gain=None):
    max_slots, rows, cols = stage_ref.shape
    k, n = dst_ref.shape
    per_band = n // cols
    n_panels = (k // rows) * per_band
    assert n == per_band * cols and k % rows == 0
    n_slots = max(d for d in range(per_band, max_slots + 1, per_band) if n_panels % d == 0)
    bands_per_group = n_slots // per_band
    n_groups = n_panels // n_slots

    def copy(g, s):
        band, col = divmod(s, per_band)
        r0 = (g * bands_per_group + band) * rows
        c0 = col * cols
        dma = pltpu.make_async_copy(w_hbm.at[layer, pl.ds(r0, rows), pl.ds(c0, cols)],
                                    stage_ref.at[s], sem.at[s])
        return dma, r0, c0

    for s in range(n_slots):
        copy(0, s)[0].start()

    def body(g, carry):
        for s in range(n_slots):
            dma, r0, c0 = copy(g, s)
            r0 = pl.multiple_of(r0, rows)
            dma.wait()
            panel = stage_ref[s]
            if row_gain is not None:
                panel = panel * row_gain(r0, rows)
            dst_ref[pl.ds(r0, rows), c0:c0 + cols] = panel.astype(BF16)

            @pl.when(g + 1 < n_groups)
            def _():
                copy(g + 1, s)[0].start()
        return carry

    lax.fori_loop(0, n_groups, body, 0)


def _split_refs(rest, n_next, n_scratch):
    n_in = n_next + (1 if n_next else 0)
    assert len(rest) == n_in + 1 + n_next + n_scratch
    return (rest[:n_next], rest[n_next] if n_next else None, rest[n_in],
            rest[n_in + 1:n_in + 1 + n_next], rest[n_in + 1 + n_next:])


def _layer_a_kernel(x_ref, gcol_ref, vg_ref, ws_ref, bias_ref, win_in, wout_in, *rest,
                    tm, d_inner, groups, layer, sub, n_next, own_weights):
    nxt_src, nxt_gain, o_ref, nxt_dst, scratch = _split_refs(rest, n_next,
                                                             7 if own_weights else 3)
    h_ref, vn_ref, y_ref = scratch[:3]
    if own_weights:
        win_ref, wout_ref, stage_ref, sem = scratch[3:]

        @pl.when(pl.program_id(0) == 0)
        def _():
            _fetch_and_round(win_in, sub, win_ref, stage_ref, sem,
                             lambda r0, rows: gcol_ref[pl.ds(r0, rows), layer:layer + 1])
            _fetch_and_round(wout_in, sub, wout_ref, stage_ref, sem)
    else:
        win_ref, wout_ref = win_in, wout_in
    _round_next_weights(nxt_src, nxt_gain, layer + 1, nxt_dst)

    x = x_ref[...]
    r = _rms_scale(x)
    v = jnp.dot(x.astype(BF16), win_ref[:, d_inner:2 * d_inner], preferred_element_type=F32)
    h_ref[...] = (x * r).astype(BF16)
    h = h_ref[...]
    v_scale = r * lax.rsqrt(r * r * _mean_sq(v) + EPS)
    vn_ref[...] = (v * v_scale * vg_ref[sub:sub + 1, :]).astype(BF16)
    gw = d_inner // groups
    for g in range(groups):
        lo = g * gw
        u = jnp.dot(h, win_ref[:, lo:lo + gw], preferred_element_type=F32)
        z = jnp.dot(h, win_ref[:, 2 * d_inner + lo:2 * d_inner + lo + gw],
                    preferred_element_type=F32)
        ws = ws_ref[g]
        for n in range(tm // CHUNK):
            r0 = n * CHUNK
            mixed = jnp.dot(ws, vn_ref[r0:r0 + CHUNK, lo:lo + gw],
                            preferred_element_type=F32) + bias_ref[:, lo:lo + gw]
            y = u[r0:r0 + CHUNK] * mixed * _silu(z[r0:r0 + CHUNK])
            y_ref[r0:r0 + CHUNK, lo:lo + gw] = y.astype(BF16)
    o_ref[...] = x + jnp.dot(y_ref[...], wout_ref[...], preferred_element_type=F32)


def _layer_b_kernel(x_ref, wc_ref, fg_ref, win_ref, wout_ref, *rest,
                    tm, d_inner, col_block, tiles_per_seq, layer, final_norm, n_next):
    nxt_src, nxt_gain, o_ref, nxt_dst, scratch = _split_refs(rest, n_next, 4)
    h_ref, xc_ref, carry_ref, y_ref = scratch

    @pl.when(pl.program_id(0) % tiles_per_seq == 0)
    def _():
        carry_ref[...] = jnp.zeros_like(carry_ref)

    x = x_ref[...]
    r = _rms_scale(x)
    for j in range(d_inner // col_block):
        cols = slice(j * col_block, (j + 1) * col_block)
        lhs = x.astype(BF16) if j == 0 else h_ref[...]
        out_scale = r if j == 0 else None

        def proj(k, lo=j * col_block, lhs=lhs, out_scale=out_scale):
            p = jnp.dot(lhs, win_ref[:, k * d_inner + lo:k * d_inner + lo + col_block],
                        preferred_element_type=F32)
            return p if out_scale is None else p * out_scale

        xc = proj(1) * proj(2)
        if j == 0:
            _round_next_weights(nxt_src, nxt_gain, layer + 1, nxt_dst)
            h_ref[...] = (x * r).astype(BF16)
        xc_ref[0:SUBLANES, :] = carry_ref[:, cols]
        xc_ref[SUBLANES:SUBLANES + tm, :] = xc
        carry_ref[:, cols] = xc[tm - SUBLANES:tm]
        conv = wc_ref[CONV_W - 1:CONV_W, cols] * xc
        for k in range(CONV_W - 1):
            shift = CONV_W - 1 - k
            conv = conv + (wc_ref[k:k + 1, cols]
                           * xc_ref[SUBLANES - shift:SUBLANES - shift + tm, :])
        y = proj(0) * conv * _silu(proj(3))
        y_ref[:, cols] = y.astype(BF16)
    out = x + jnp.dot(y_ref[...], wout_ref[...], preferred_element_type=F32)
    if final_norm:
        out = out * _rms_scale(out) * fg_ref[...]
    o_ref[...] = out


def _resident(a, layer=None):
    if layer is None:
        return pl.BlockSpec(a.shape, lambda i: (0,) * a.ndim, pipeline_mode=pl.Buffered(1))
    return pl.BlockSpec((None,) + a.shape[1:], lambda i: (layer,) + (0,) * (a.ndim - 1),
                        pipeline_mode=pl.Buffered(1))


def _slab_rows(w, n_steps):
    rows = w.shape[-2] // n_steps
    assert rows * n_steps == w.shape[-2] and rows % BF16_ROWS == 0
    return rows


def _call_layer(kern, x2, args, in_specs, next_weights, gain_cols, scratch_shapes, *, tm, name):
    n_tok, d_model = x2.shape
    n_steps = n_tok // tm
    row_tile = pl.BlockSpec((tm, d_model), lambda i: (i, 0))
    slabs_in = [pl.BlockSpec((None, _slab_rows(w, n_steps), w.shape[2]),
                             lambda i, layer=layer: (layer, i, 0)) for w, layer in next_weights]
    slabs_out = [pl.BlockSpec((_slab_rows(w, n_steps), w.shape[2]), lambda i: (i, 0))
                 for w, _ in next_weights]
    slab_args = [w for w, _ in next_weights]
    if next_weights:
        slabs_in.append(pl.BlockSpec((_slab_rows(next_weights[0][0], n_steps), gain_cols.shape[1]),
                                     lambda i: (i, 0)))
        slab_args.append(gain_cols)
    outs = pl.pallas_call(
        functools.partial(kern, tm=tm, n_next=len(next_weights)),
        grid=(n_steps,),
        in_specs=[row_tile] + in_specs + slabs_in,
        out_specs=[row_tile] + slabs_out,
        out_shape=[jax.ShapeDtypeStruct(x2.shape, x2.dtype)]
                  + [jax.ShapeDtypeStruct(w.shape[1:], BF16) for w, _ in next_weights],
        scratch_shapes=scratch_shapes,
        compiler_params=pltpu.CompilerParams(dimension_semantics=("arbitrary",),
                                             vmem_limit_bytes=VMEM_LIMIT_BYTES),
        name=name,
    )(x2, *args, *slab_args)
    return outs[0], outs[1:]


def _layer_a(x2, gain_cols, v_g, ws, bias, w_in, w_out, next_weights, *,
             tm, layer, sub, own_weights):
    d_model = x2.shape[1]
    d_inner = w_out.shape[-2]
    kern = functools.partial(_layer_a_kernel, d_inner=d_inner, groups=ws.shape[1],
                             layer=layer, sub=sub, own_weights=own_weights)
    scratch = [pltpu.VMEM((tm, d_model), BF16),
               pltpu.VMEM((tm, d_inner), BF16),
               pltpu.VMEM((tm, d_inner), BF16)]
    if own_weights:
        w_specs = [pl.BlockSpec(memory_space=pl.ANY)] * 2
        scratch += [pltpu.VMEM(w_in.shape[1:], BF16),
                    pltpu.VMEM(w_out.shape[1:], BF16),
                    pltpu.VMEM(STAGE_SHAPE, F32),
                    pltpu.SemaphoreType.DMA(STAGE_SHAPE[:1])]
    else:
        w_specs = [_resident(w_in), _resident(w_out)]
    return _call_layer(
        kern, x2,
        [gain_cols, v_g, ws, bias, w_in, w_out],
        [_resident(gain_cols), _resident(v_g), _resident(ws, sub), _resident(bias, sub)] + w_specs,
        next_weights, gain_cols, scratch, tm=tm, name="gmlp_layer")


def _layer_b(x2, gain_cols, w_conv, final_g, w_in, w_out, next_weights, *,
             tm, seq, col_block, layer, sub, final_norm):
    d_model = x2.shape[1]
    d_inner = w_out.shape[0]
    kern = functools.partial(_layer_b_kernel, d_inner=d_inner, col_block=col_block,
                             tiles_per_seq=seq // tm, layer=layer, final_norm=final_norm)
    return _call_layer(
        kern, x2,
        [w_conv, final_g, w_in, w_out],
        [_resident(w_conv, sub), _resident(final_g), _resident(w_in), _resident(w_out)],
        next_weights, gain_cols,
        [pltpu.VMEM((tm, d_model), BF16),
         pltpu.VMEM((tm + SUBLANES, col_block), F32),
         pltpu.VMEM((SUBLANES, d_inner), F32),
         pltpu.VMEM((tm, d_inner), BF16)],
        tm=tm, name="shortconv_layer")


def kernel(x, norm_g, final_g, a_w_in, a_v_norm_g, a_w_s, a_b_s, a_w_out, b_w_in, b_w_conv, b_w_out):
    batch, seq, d_model = x.shape
    depth = norm_g.shape[0]
    d_inner = a_w_out.shape[1]
    tm = 1024
    assert seq % tm == 0 and tm % CHUNK == 0
    assert depth % 2 == 0, "the final norm is fused into the last short-conv layer"
    x2 = x.reshape(batch * seq, d_model)
    causal = jnp.tril(jnp.ones((CHUNK, CHUNK), dtype=bool))
    ws = jnp.where(causal, a_w_s, jnp.zeros((), a_w_s.dtype)).astype(BF16)
    bias = jnp.repeat(jnp.swapaxes(a_b_s, 1, 2), d_inner // a_w_s.shape[1], axis=2)
    gain_cols = jnp.transpose(norm_g)
    final_g = final_g.reshape(1, d_model)
    w_in, w_out = a_w_in, a_w_out
    for i in range(depth):
        j = i // 2
        if i % 2 == 0:
            x2, (w_in, w_out) = _layer_a(
                x2, gain_cols, a_v_norm_g, ws, bias, w_in, w_out,
                ((b_w_in, j), (b_w_out, j)), tm=tm, layer=i, sub=j, own_weights=i == 0)
        else:
            last = i == depth - 1
            x2, nxt = _layer_b(
                x2, gain_cols, b_w_conv, final_g, w_in, w_out,
                () if last else ((a_w_in, j + 1), (a_w_out, j + 1)),
                tm=tm, seq=seq, col_block=256, layer=i, sub=j, final_norm=last)
            if not last:
                w_in, w_out = nxt
    return x2.reshape(batch, seq, d_model)
```

```python
import functools

import jax
import jax.numpy as jnp
from jax import lax
from jax.experimental import pallas as pl
from jax.experimental.pallas import tpu as pltpu

EPS = 1e-6
CHUNK = 128
CONV_W = 3
SUBLANES = 8
BF16_ROWS = 16
XC_HEAD = 16
VMEM_LIMIT_BYTES = 60 * 1024 * 1024
STAGE_SHAPE = (6, 256, 1024)

F32 = jnp.float32
BF16 = jnp.bfloat16


def _rms_scale(x):
    return lax.rsqrt(jnp.mean(x * x, axis=-1, keepdims=True) + EPS)


def _silu(z):
    return z * jax.nn.sigmoid(z)


def _round_next_weights(srcs, dsts):
    for src, dst in zip(srcs, dsts):
        dst[...] = src[...].astype(BF16)


def _fetch_and_round(w_hbm, layer, dst_ref, stage_ref, sem):
    max_slots, rows, cols = stage_ref.shape
    k, n = dst_ref.shape
    per_band = n // cols
    n_panels = (k // rows) * per_band
    assert n == per_band * cols and k % rows == 0
    n_slots = max(d for d in range(per_band, max_slots + 1, per_band) if n_panels % d == 0)
    bands_per_group = n_slots // per_band
    n_groups = n_panels // n_slots

    def copy(g, s):
        band, col = divmod(s, per_band)
        r0 = (g * bands_per_group + band) * rows
        c0 = col * cols
        dma = pltpu.make_async_copy(w_hbm.at[layer, pl.ds(r0, rows), pl.ds(c0, cols)],
                                    stage_ref.at[s], sem.at[s])
        return dma, r0, c0

    for s in range(n_slots):
        copy(0, s)[0].start()

    def body(g, carry):
        for s in range(n_slots):
            dma, r0, c0 = copy(g, s)
            dma.wait()
            dst_ref[pl.ds(pl.multiple_of(r0, rows), rows), c0:c0 + cols] = (
                stage_ref[s].astype(BF16))

            @pl.when(g + 1 < n_groups)
            def _():
                copy(g + 1, s)[0].start()
        return carry

    lax.fori_loop(0, n_groups, body, 0)


def _split_refs(rest, n_next, n_scratch):
    assert len(rest) == 2 * n_next + 1 + n_scratch
    return rest[:n_next], rest[n_next], rest[n_next + 1:2 * n_next + 1], rest[2 * n_next + 1:]


def _prepare_mixing(ws_in, bs_in, ws_ref, bias_ref):
    groups = ws_in.shape[0]
    gw = bias_ref.shape[1] // groups
    t = lax.broadcasted_iota(jnp.int32, (CHUNK, CHUNK), 0)
    s = lax.broadcasted_iota(jnp.int32, (CHUNK, CHUNK), 1)
    for g in range(groups):
        ws_ref[g] = jnp.where(t >= s, ws_in[g], 0.0).astype(BF16)
        b_col = jnp.sum(jnp.where(t == s, bs_in[g:g + 1, :], 0.0), axis=-1, keepdims=True)
        bias_ref[:, g * gw:(g + 1) * gw] = jnp.broadcast_to(b_col, (CHUNK, gw))


def _layer_a_kernel(x_ref, g_ref, vg_ref, ws_in, bs_in, win_in, wout_in, *rest,
                    tm, d_inner, groups, layer, sub, n_next, own_weights):
    nxt_src, o_ref, nxt_dst, scratch = _split_refs(rest, n_next, 9 if own_weights else 5)
    h_ref, vn_ref, y_ref, ws_ref, bias_ref = scratch[:5]
    if own_weights:
        win_ref, wout_ref, stage_ref, sem = scratch[5:]
    else:
        win_ref, wout_ref = win_in, wout_in

    @pl.when(pl.program_id(0) == 0)
    def _():
        _prepare_mixing(ws_in, bs_in, ws_ref, bias_ref)
        if own_weights:
            _fetch_and_round(win_in, sub, win_ref, stage_ref, sem)
            _fetch_and_round(wout_in, sub, wout_ref, stage_ref, sem)

    _round_next_weights(nxt_src, nxt_dst)

    x = x_ref[...]
    h_ref[...] = (x * _rms_scale(x) * g_ref[layer:layer + 1, :]).astype(BF16)
    h = h_ref[...]
    v = jnp.dot(h, win_ref[:, d_inner:2 * d_inner], preferred_element_type=F32)
    vn_ref[...] = (v * _rms_scale(v) * vg_ref[sub:sub + 1, :]).astype(BF16)
    gw = d_inner // groups
    for g in range(groups):
        lo = g * gw
        u = jnp.dot(h, win_ref[:, lo:lo + gw], preferred_element_type=F32)
        z = jnp.dot(h, win_ref[:, 2 * d_inner + lo:2 * d_inner + lo + gw],
                    preferred_element_type=F32)
        ws = ws_ref[g]
        for n in range(tm // CHUNK):
            r0 = n * CHUNK
            mixed = jnp.dot(ws, vn_ref[r0:r0 + CHUNK, lo:lo + gw],
                            preferred_element_type=F32) + bias_ref[:, lo:lo + gw]
            y = u[r0:r0 + CHUNK] * mixed * _silu(z[r0:r0 + CHUNK])
            y_ref[r0:r0 + CHUNK, lo:lo + gw] = y.astype(BF16)
    o_ref[...] = x + jnp.dot(y_ref[...], wout_ref[...], preferred_element_type=F32)


def _layer_b_kernel(x_ref, g_ref, wc_ref, fg_ref, win_ref, wout_ref, *rest,
                    tm, d_inner, col_block, tiles_per_seq, layer, final_norm, n_next):
    nxt_src, o_ref, nxt_dst, (h_ref, xc_ref, carry_ref, y_ref) = _split_refs(rest, n_next, 4)

    @pl.when(pl.program_id(0) % tiles_per_seq == 0)
    def _():
        carry_ref[...] = jnp.zeros_like(carry_ref)

    _round_next_weights(nxt_src, nxt_dst)
    x = x_ref[...]
    h_ref[...] = (x * _rms_scale(x) * g_ref[layer:layer + 1, :]).astype(BF16)
    h = h_ref[...]
    for j in range(d_inner // col_block):
        cols = slice(j * col_block, (j + 1) * col_block)

        def proj(k, lo=j * col_block):
            return jnp.dot(h, win_ref[:, k * d_inner + lo:k * d_inner + lo + col_block],
                           preferred_element_type=F32)

        xc = proj(1) * proj(2)
        xc_ref[XC_HEAD - SUBLANES:XC_HEAD, :] = carry_ref[:, cols]
        xc_ref[XC_HEAD:XC_HEAD + tm, :] = xc
        carry_ref[:, cols] = xc[tm - SUBLANES:tm]
        conv = wc_ref[CONV_W - 1:CONV_W, cols] * xc
        for k in range(CONV_W - 1):
            shift = CONV_W - 1 - k
            conv = conv + (wc_ref[k:k + 1, cols]
                           * xc_ref[XC_HEAD - shift:XC_HEAD - shift + tm, :])
        y = proj(0) * conv * _silu(proj(3))
        y_ref[:, cols] = y.astype(BF16)
    out = x + jnp.dot(y_ref[...], wout_ref[...], preferred_element_type=F32)
    if final_norm:
        out = out * _rms_scale(out) * fg_ref[...]
    o_ref[...] = out


def _resident(a, layer=None):
    if layer is None:
        return pl.BlockSpec(a.shape, lambda i: (0,) * a.ndim, pipeline_mode=pl.Buffered(1))
    return pl.BlockSpec((None,) + a.shape[1:], lambda i: (layer,) + (0,) * (a.ndim - 1),
                        pipeline_mode=pl.Buffered(1))


def _slab_rows(w, n_steps):
    rows = w.shape[-2] // n_steps
    assert rows * n_steps == w.shape[-2] and rows % BF16_ROWS == 0
    return rows


def _call_layer(kern, x2, args, in_specs, next_weights, scratch_shapes, *, tm, name):
    n_tok, d_model = x2.shape
    n_steps = n_tok // tm
    row_tile = pl.BlockSpec((tm, d_model), lambda i: (i, 0))
    slabs_in = [pl.BlockSpec((None, _slab_rows(w, n_steps), w.shape[2]),
                             lambda i, layer=layer: (layer, i, 0)) for w, layer in next_weights]
    slabs_out = [pl.BlockSpec((_slab_rows(w, n_steps), w.shape[2]), lambda i: (i, 0))
                 for w, _ in next_weights]
    outs = pl.pallas_call(
        functools.partial(kern, tm=tm, n_next=len(next_weights)),
        grid=(n_steps,),
        in_specs=[row_tile] + in_specs + slabs_in,
        out_specs=[row_tile] + slabs_out,
        out_shape=[jax.ShapeDtypeStruct(x2.shape, x2.dtype)]
                  + [jax.ShapeDtypeStruct(w.shape[1:], BF16) for w, _ in next_weights],
        scratch_shapes=scratch_shapes,
        compiler_params=pltpu.CompilerParams(dimension_semantics=("arbitrary",),
                                             vmem_limit_bytes=VMEM_LIMIT_BYTES),
        name=name,
    )(x2, *args, *[w for w, _ in next_weights])
    return outs[0], outs[1:]


def _layer_a(x2, norm_g, v_g, w_s, b_s, w_in, w_out, next_weights, *,
             tm, layer, sub, own_weights):
    d_model = x2.shape[1]
    d_inner = w_out.shape[-2]
    kern = functools.partial(_layer_a_kernel, d_inner=d_inner, groups=w_s.shape[1],
                             layer=layer, sub=sub, own_weights=own_weights)
    scratch = [pltpu.VMEM((tm, d_model), BF16),
               pltpu.VMEM((tm, d_inner), BF16),
               pltpu.VMEM((tm, d_inner), BF16),
               pltpu.VMEM(w_s.shape[1:], BF16),
               pltpu.VMEM((CHUNK, d_inner), F32)]
    if own_weights:
        w_specs = [pl.BlockSpec(memory_space=pl.ANY)] * 2
        scratch += [pltpu.VMEM(w_in.shape[1:], BF16),
                    pltpu.VMEM(w_out.shape[1:], BF16),
                    pltpu.VMEM(STAGE_SHAPE, F32),
                    pltpu.SemaphoreType.DMA(STAGE_SHAPE[:1])]
    else:
        w_specs = [_resident(w_in), _resident(w_out)]
    return _call_layer(
        kern, x2,
        [norm_g, v_g, w_s, b_s, w_in, w_out],
        [_resident(norm_g), _resident(v_g), _resident(w_s, sub), _resident(b_s, sub)] + w_specs,
        next_weights, scratch, tm=tm, name="gmlp_layer")


def _layer_b(x2, norm_g, w_conv, final_g, w_in, w_out, next_weights, *,
             tm, seq, col_block, layer, sub, final_norm):
    d_model = x2.shape[1]
    d_inner = w_out.shape[0]
    kern = functools.partial(_layer_b_kernel, d_inner=d_inner, col_block=col_block,
                             tiles_per_seq=seq // tm, layer=layer, final_norm=final_norm)
    return _call_layer(
        kern, x2,
        [norm_g, w_conv, final_g, w_in, w_out],
        [_resident(norm_g), _resident(w_conv, sub), _resident(final_g),
         _resident(w_in), _resident(w_out)],
        next_weights,
        [pltpu.VMEM((tm, d_model), BF16),
         pltpu.VMEM((tm + XC_HEAD, col_block), F32),
         pltpu.VMEM((SUBLANES, d_inner), F32),
         pltpu.VMEM((tm, d_inner), BF16)],
        tm=tm, name="shortconv_layer")


def kernel(x, norm_g, final_g, a_w_in, a_v_norm_g, a_w_s, a_b_s, a_w_out, b_w_in, b_w_conv, b_w_out):
    batch, seq, d_model = x.shape
    depth = norm_g.shape[0]
    tm = 1024
    assert seq % tm == 0 and tm % CHUNK == 0
    assert depth % 2 == 0, "the final norm is fused into the last short-conv layer"
    x2 = x.reshape(batch * seq, d_model)
    assert a_w_s.shape[2:] == (CHUNK, CHUNK) and a_b_s.shape[2] == CHUNK
    final_g = final_g.reshape(1, d_model)
    w_in, w_out = a_w_in, a_w_out
    for i in range(depth):
        j = i // 2
        if i % 2 == 0:
            x2, (w_in, w_out) = _layer_a(
                x2, norm_g, a_v_norm_g, a_w_s, a_b_s, w_in, w_out,
                [(b_w_in, j), (b_w_out, j)], tm=tm, layer=i, sub=j, own_weights=i == 0)
        else:
            last = i == depth - 1
            x2, nxt = _layer_b(
                x2, norm_g, b_w_conv, final_g, w_in, w_out,
                [] if last else [(a_w_in, j + 1), (a_w_out, j + 1)],
                tm=tm, seq=seq, col_block=256, layer=i, sub=j, final_norm=last)
            if not last:
                w_in, w_out = nxt
    return x2.reshape(batch, seq, d_model)
```

```python
import functools

import jax
import jax.numpy as jnp
from jax import lax
from jax.experimental import pallas as pl
from jax.experimental.pallas import tpu as pltpu

EPS = 1e-6
CHUNK = 128
CONV_W = 3
SUBLANES = 8
BF16_ROWS = 16
XC_HEAD = 16
VMEM_LIMIT_BYTES = 60 * 1024 * 1024
STAGE_SHAPE = (6, 256, 1024)

F32 = jnp.float32
BF16 = jnp.bfloat16


def _rms_scale(x):
    return lax.rsqrt(jnp.mean(x * x, axis=-1, keepdims=True) + EPS)


def _silu(z):
    return z * jax.nn.sigmoid(z)


def _round_next_weights(srcs, dsts):
    for src, dst in zip(srcs, dsts):
        dst[...] = src[...].astype(BF16)


def _fetch_and_round(w_hbm, layer, dst_ref, stage_ref, sem):
    max_slots, rows, cols = stage_ref.shape
    k, n = dst_ref.shape
    per_band = n // cols
    n_panels = (k // rows) * per_band
    assert n == per_band * cols and k % rows == 0
    n_slots = max(d for d in range(per_band, max_slots + 1, per_band) if n_panels % d == 0)
    bands_per_group = n_slots // per_band
    n_groups = n_panels // n_slots

    def copy(g, s):
        band, col = divmod(s, per_band)
        r0 = (g * bands_per_group + band) * rows
        c0 = col * cols
        dma = pltpu.make_async_copy(w_hbm.at[layer, pl.ds(r0, rows), pl.ds(c0, cols)],
                                    stage_ref.at[s], sem.at[s])
        return dma, r0, c0

    for s in range(n_slots):
        copy(0, s)[0].start()

    def body(g, carry):
        for s in range(n_slots):
            dma, r0, c0 = copy(g, s)
            dma.wait()
            dst_ref[pl.ds(pl.multiple_of(r0, rows), rows), c0:c0 + cols] = (
                stage_ref[s].astype(BF16))

            @pl.when(g + 1 < n_groups)
            def _():
                copy(g + 1, s)[0].start()
        return carry

    lax.fori_loop(0, n_groups, body, 0)


def _split_refs(rest, n_next, n_scratch):
    assert len(rest) == 2 * n_next + 1 + n_scratch
    return rest[:n_next], rest[n_next], rest[n_next + 1:2 * n_next + 1], rest[2 * n_next + 1:]


def _prepare_mixing(ws_in, bs_in, ws_ref, bias_ref):
    groups = ws_in.shape[0]
    gw = bias_ref.shape[1] // groups
    t = lax.broadcasted_iota(jnp.int32, (CHUNK, CHUNK), 0)
    s = lax.broadcasted_iota(jnp.int32, (CHUNK, CHUNK), 1)
    for g in range(groups):
        ws_ref[g] = jnp.where(t >= s, ws_in[g], 0.0).astype(BF16)
        b_col = jnp.sum(jnp.where(t == s, bs_in[g:g + 1, :], 0.0), axis=-1, keepdims=True)
        bias_ref[:, g * gw:(g + 1) * gw] = jnp.broadcast_to(b_col, (CHUNK, gw))


def _layer_a_kernel(x_ref, g_ref, vg_ref, ws_in, bs_in, win_in, wout_in, *rest,
                    tm, d_inner, groups, layer, sub, n_next, own_weights):
    nxt_src, o_ref, nxt_dst, scratch = _split_refs(rest, n_next, 9 if own_weights else 5)
    h_ref, vn_ref, y_ref, ws_ref, bias_ref = scratch[:5]
    if own_weights:
        win_ref, wout_ref, stage_ref, sem = scratch[5:]
    else:
        win_ref, wout_ref = win_in, wout_in

    @pl.when(pl.program_id(0) == 0)
    def _():
        _prepare_mixing(ws_in, bs_in, ws_ref, bias_ref)
        if own_weights:
            _fetch_and_round(win_in, sub, win_ref, stage_ref, sem)
            _fetch_and_round(wout_in, sub, wout_ref, stage_ref, sem)

    _round_next_weights(nxt_src, nxt_dst)

    x = x_ref[...]
    h_ref[...] = (x * _rms_scale(x) * g_ref[layer:layer + 1, :]).astype(BF16)
    h = h_ref[...]
    v = jnp.dot(h, win_ref[:, d_inner:2 * d_inner], preferred_element_type=F32)
    vn_ref[...] = (v * _rms_scale(v) * vg_ref[sub:sub + 1, :]).astype(BF16)
    gw = d_inner // groups
    for g in range(groups):
        lo = g * gw
        u = jnp.dot(h, win_ref[:, lo:lo + gw], preferred_element_type=F32)
        z = jnp.dot(h, win_ref[:, 2 * d_inner + lo:2 * d_inner + lo + gw],
                    preferred_element_type=F32)
        ws = ws_ref[g]
        for n in range(tm // CHUNK):
            r0 = n * CHUNK
            mixed = jnp.dot(ws, vn_ref[r0:r0 + CHUNK, lo:lo + gw],
                            preferred_element_type=F32) + bias_ref[:, lo:lo + gw]
            y = u[r0:r0 + CHUNK] * mixed * _silu(z[r0:r0 + CHUNK])
            y_ref[r0:r0 + CHUNK, lo:lo + gw] = y.astype(BF16)
    o_ref[...] = x + jnp.dot(y_ref[...], wout_ref[...], preferred_element_type=F32)


def _layer_b_kernel(x_ref, g_ref, wc_ref, fg_ref, win_ref, wout_ref, *rest,
                    tm, d_inner, col_block, tiles_per_seq, layer, final_norm, n_next):
    nxt_src, o_ref, nxt_dst, (h_ref, xc_ref, carry_ref, y_ref) = _split_refs(rest, n_next, 4)

    @pl.when(pl.program_id(0) % tiles_per_seq == 0)
    def _():
        carry_ref[...] = jnp.zeros_like(carry_ref)

    def main():
        _round_next_weights(nxt_src, nxt_dst)
        x = x_ref[...]
        h_ref[...] = (x * _rms_scale(x) * g_ref[layer:layer + 1, :]).astype(BF16)
        h = h_ref[...]
        for j in range(d_inner // col_block):
            cols = slice(j * col_block, (j + 1) * col_block)

            def proj(k, lo=j * col_block):
                return jnp.dot(h, win_ref[:, k * d_inner + lo:k * d_inner + lo + col_block],
                               preferred_element_type=F32)

            xc = proj(1) * proj(2)
            xc_ref[XC_HEAD - SUBLANES:XC_HEAD, :] = carry_ref[:, cols]
            xc_ref[XC_HEAD:XC_HEAD + tm, :] = xc
            carry_ref[:, cols] = xc[tm - SUBLANES:tm]
            conv = wc_ref[CONV_W - 1:CONV_W, cols] * xc
            for k in range(CONV_W - 1):
                shift = CONV_W - 1 - k
                conv = conv + (wc_ref[k:k + 1, cols]
                               * xc_ref[XC_HEAD - shift:XC_HEAD - shift + tm, :])
            y = proj(0) * conv * _silu(proj(3))
            y_ref[:, cols] = y.astype(BF16)
        out = x + jnp.dot(y_ref[...], wout_ref[...], preferred_element_type=F32)
        if final_norm:
            out = out * _rms_scale(out) * fg_ref[...]
        o_ref[...] = out

    pl.when(pl.program_id(0) == 0)(main)
    pl.when(pl.program_id(0) != 0)(main)


def _resident(a, layer=None):
    if layer is None:
        return pl.BlockSpec(a.shape, lambda i: (0,) * a.ndim, pipeline_mode=pl.Buffered(1))
    return pl.BlockSpec((None,) + a.shape[1:], lambda i: (layer,) + (0,) * (a.ndim - 1),
                        pipeline_mode=pl.Buffered(1))


def _slab_rows(w, n_steps):
    rows = w.shape[-2] // n_steps
    assert rows * n_steps == w.shape[-2] and rows % BF16_ROWS == 0
    return rows


def _call_layer(kern, x2, args, in_specs, next_weights, scratch_shapes, *, tm, name):
    n_tok, d_model = x2.shape
    n_steps = n_tok // tm
    row_tile = pl.BlockSpec((tm, d_model), lambda i: (i, 0))
    slabs_in = [pl.BlockSpec((None, _slab_rows(w, n_steps), w.shape[2]),
                             lambda i, layer=layer: (layer, i, 0)) for w, layer in next_weights]
    slabs_out = [pl.BlockSpec((_slab_rows(w, n_steps), w.shape[2]), lambda i: (i, 0))
                 for w, _ in next_weights]
    outs = pl.pallas_call(
        functools.partial(kern, tm=tm, n_next=len(next_weights)),
        grid=(n_steps,),
        in_specs=[row_tile] + in_specs + slabs_in,
        out_specs=[row_tile] + slabs_out,
        out_shape=[jax.ShapeDtypeStruct(x2.shape, x2.dtype)]
                  + [jax.ShapeDtypeStruct(w.shape[1:], BF16) for w, _ in next_weights],
        scratch_shapes=scratch_shapes,
        compiler_params=pltpu.CompilerParams(dimension_semantics=("arbitrary",),
                                             vmem_limit_bytes=VMEM_LIMIT_BYTES),
        name=name,
    )(x2, *args, *[w for w, _ in next_weights])
    return outs[0], outs[1:]


def _layer_a(x2, norm_g, v_g, w_s, b_s, w_in, w_out, next_weights, *,
             tm, layer, sub, own_weights):
    d_model = x2.shape[1]
    d_inner = w_out.shape[-2]
    kern = functools.partial(_layer_a_kernel, d_inner=d_inner, groups=w_s.shape[1],
                             layer=layer, sub=sub, own_weights=own_weights)
    scratch = [pltpu.VMEM((tm, d_model), BF16),
               pltpu.VMEM((tm, d_inner), BF16),
               pltpu.VMEM((tm, d_inner), BF16),
               pltpu.VMEM(w_s.shape[1:], BF16),
               pltpu.VMEM((CHUNK, d_inner), F32)]
    if own_weights:
        w_specs = [pl.BlockSpec(memory_space=pl.ANY)] * 2
        scratch += [pltpu.VMEM(w_in.shape[1:], BF16),
                    pltpu.VMEM(w_out.shape[1:], BF16),
                    pltpu.VMEM(STAGE_SHAPE, F32),
                    pltpu.SemaphoreType.DMA(STAGE_SHAPE[:1])]
    else:
        w_specs = [_resident(w_in), _resident(w_out)]
    return _call_layer(
        kern, x2,
        [norm_g, v_g, w_s, b_s, w_in, w_out],
        [_resident(norm_g), _resident(v_g), _resident(w_s, sub), _resident(b_s, sub)] + w_specs,
        next_weights, scratch, tm=tm, name="gmlp_layer")


def _layer_b(x2, norm_g, w_conv, final_g, w_in, w_out, next_weights, *,
             tm, seq, col_block, layer, sub, final_norm):
    d_model = x2.shape[1]
    d_inner = w_out.shape[0]
    kern = functools.partial(_layer_b_kernel, d_inner=d_inner, col_block=col_block,
                             tiles_per_seq=seq // tm, layer=layer, final_norm=final_norm)
    return _call_layer(
        kern, x2,
        [norm_g, w_conv, final_g, w_in, w_out],
        [_resident(norm_g), _resident(w_conv, sub), _resident(final_g),
         _resident(w_in), _resident(w_out)],
        next_weights,
        [pltpu.VMEM((tm, d_model), BF16),
         pltpu.VMEM((tm + XC_HEAD, col_block), F32),
         pltpu.VMEM((SUBLANES, d_inner), F32),
         pltpu.VMEM((tm, d_inner), BF16)],
        tm=tm, name="shortconv_layer")


def kernel(x, norm_g, final_g, a_w_in, a_v_norm_g, a_w_s, a_b_s, a_w_out, b_w_in, b_w_conv, b_w_out):
    batch, seq, d_model = x.shape
    depth = norm_g.shape[0]
    tm = 1024
    assert seq % tm == 0 and tm % CHUNK == 0
    assert depth % 2 == 0, "the final norm is fused into the last short-conv layer"
    x2 = x.reshape(batch * seq, d_model)
    assert a_w_s.shape[2:] == (CHUNK, CHUNK) and a_b_s.shape[2] == CHUNK
    final_g = final_g.reshape(1, d_model)
    w_in, w_out = a_w_in, a_w_out
    for i in range(depth):
        j = i // 2
        if i % 2 == 0:
            x2, (w_in, w_out) = _layer_a(
                x2, norm_g, a_v_norm_g, a_w_s, a_b_s, w_in, w_out,
                [(b_w_in, j), (b_w_out, j)], tm=tm, layer=i, sub=j, own_weights=i == 0)
        else:
            last = i == depth - 1
            x2, nxt = _layer_b(
                x2, norm_g, b_w_conv, final_g, w_in, w_out,
                [] if last else [(a_w_in, j + 1), (a_w_out, j + 1)],
                tm=tm, seq=seq, col_block=256, layer=i, sub=j, final_norm=last)
            if not last:
                w_in, w_out = nxt
    return x2.reshape(batch, seq, d_model)
```

```python
import functools

import jax
import jax.numpy as jnp
from jax import lax
from jax.experimental import pallas as pl
from jax.experimental.pallas import tpu as pltpu

EPS = 1e-6
CHUNK = 128
CONV_W = 3
SUBLANES = 8
BF16_ROWS = 16
XC_HEAD = 16
VMEM_LIMIT_BYTES = 60 * 1024 * 1024
STAGE_SHAPE = (6, 256, 1024)

F32 = jnp.float32
BF16 = jnp.bfloat16


def _rms_scale(x):
    return lax.rsqrt(jnp.mean(x * x, axis=-1, keepdims=True) + EPS)


def _silu(z):
    return z * jax.nn.sigmoid(z)


def _round_next_weights(srcs, dsts):
    for src, dst in zip(srcs, dsts):
        dst[...] = src[...].astype(BF16)


def _fetch_and_round(w_hbm, layer, dst_ref, stage_ref, sem):
    max_slots, rows, cols = stage_ref.shape
    k, n = dst_ref.shape
    per_band = n // cols
    n_panels = (k // rows) * per_band
    assert n == per_band * cols and k % rows == 0
    n_slots = max(d for d in range(per_band, max_slots + 1, per_band) if n_panels % d == 0)
    bands_per_group = n_slots // per_band
    n_groups = n_panels // n_slots

    def copy(g, s):
        band, col = divmod(s, per_band)
        r0 = (g * bands_per_group + band) * rows
        c0 = col * cols
        dma = pltpu.make_async_copy(w_hbm.at[layer, pl.ds(r0, rows), pl.ds(c0, cols)],
                                    stage_ref.at[s], sem.at[s])
        return dma, r0, c0

    for s in range(n_slots):
        copy(0, s)[0].start()

    def body(g, carry):
        for s in range(n_slots):
            dma, r0, c0 = copy(g, s)
            dma.wait()
            dst_ref[pl.ds(pl.multiple_of(r0, rows), rows), c0:c0 + cols] = (
                stage_ref[s].astype(BF16))

            @pl.when(g + 1 < n_groups)
            def _():
                copy(g + 1, s)[0].start()
        return carry

    lax.fori_loop(0, n_groups, body, 0)


def _split_refs(rest, n_next, n_scratch):
    n_out = 1 + (1 if n_next else 0)
    assert len(rest) == 2 * n_next + n_out + n_scratch
    r_out = rest[n_next + 1] if n_next else None
    return (rest[:n_next], rest[n_next], r_out, rest[n_next + n_out:2 * n_next + n_out],
            rest[2 * n_next + n_out:])


def _prepare_mixing(ws_in, bs_in, ws_ref, bias_ref):
    groups = ws_in.shape[0]
    gw = bias_ref.shape[1] // groups
    t = lax.broadcasted_iota(jnp.int32, (CHUNK, CHUNK), 0)
    s = lax.broadcasted_iota(jnp.int32, (CHUNK, CHUNK), 1)
    for g in range(groups):
        ws_ref[g] = jnp.where(t >= s, ws_in[g], 0.0).astype(BF16)
        b_col = jnp.sum(jnp.where(t == s, bs_in[g:g + 1, :], 0.0), axis=-1, keepdims=True)
        bias_ref[:, g * gw:(g + 1) * gw] = jnp.broadcast_to(b_col, (CHUNK, gw))


def _layer_a_kernel(x_ref, *refs, tm, d_inner, groups, layer, sub, n_next, own_weights, has_scale):
    r_ref, refs = (refs[0], refs[1:]) if has_scale else (None, refs)
    (g_ref, vg_ref, ws_in, bs_in, win_in, wout_in), rest = refs[:6], refs[6:]
    nxt_src, o_ref, r_out, nxt_dst, scratch = _split_refs(rest, n_next, 9 if own_weights else 5)
    h_ref, vn_ref, y_ref, ws_ref, bias_ref = scratch[:5]
    if own_weights:
        win_ref, wout_ref, stage_ref, sem = scratch[5:]
    else:
        win_ref, wout_ref = win_in, wout_in

    @pl.when(pl.program_id(0) == 0)
    def _():
        _prepare_mixing(ws_in, bs_in, ws_ref, bias_ref)
        if own_weights:
            _fetch_and_round(win_in, sub, win_ref, stage_ref, sem)
            _fetch_and_round(wout_in, sub, wout_ref, stage_ref, sem)

    _round_next_weights(nxt_src, nxt_dst)

    x = x_ref[...]
    r = r_ref[...] if has_scale else _rms_scale(x)
    h_ref[...] = (x * r * g_ref[layer:layer + 1, :]).astype(BF16)
    h = h_ref[...]
    v = jnp.dot(h, win_ref[:, d_inner:2 * d_inner], preferred_element_type=F32)
    vn_ref[...] = (v * _rms_scale(v) * vg_ref[sub:sub + 1, :]).astype(BF16)
    gw = d_inner // groups
    for g in range(groups):
        lo = g * gw
        u = jnp.dot(h, win_ref[:, lo:lo + gw], preferred_element_type=F32)
        z = jnp.dot(h, win_ref[:, 2 * d_inner + lo:2 * d_inner + lo + gw],
                    preferred_element_type=F32)
        ws = ws_ref[g]
        for n in range(tm // CHUNK):
            r0 = n * CHUNK
            mixed = jnp.dot(ws, vn_ref[r0:r0 + CHUNK, lo:lo + gw],
                            preferred_element_type=F32) + bias_ref[:, lo:lo + gw]
            y = u[r0:r0 + CHUNK] * mixed * _silu(z[r0:r0 + CHUNK])
            y_ref[r0:r0 + CHUNK, lo:lo + gw] = y.astype(BF16)
    out = x + jnp.dot(y_ref[...], wout_ref[...], preferred_element_type=F32)
    if r_out is not None:
        r_out[...] = _rms_scale(out)
    o_ref[...] = out


def _layer_b_kernel(x_ref, *refs, tm, d_inner, col_block, tiles_per_seq, layer, final_norm, n_next,
                    has_scale):
    r_ref, refs = (refs[0], refs[1:]) if has_scale else (None, refs)
    (g_ref, wc_ref, fg_ref, win_ref, wout_ref), rest = refs[:5], refs[5:]
    nxt_src, o_ref, r_out, nxt_dst, (h_ref, xc_ref, carry_ref, y_ref) = _split_refs(rest, n_next, 4)

    @pl.when(pl.program_id(0) % tiles_per_seq == 0)
    def _():
        carry_ref[...] = jnp.zeros_like(carry_ref)

    _round_next_weights(nxt_src, nxt_dst)
    x = x_ref[...]
    r = r_ref[...] if has_scale else _rms_scale(x)
    h_ref[...] = (x * r * g_ref[layer:layer + 1, :]).astype(BF16)
    h = h_ref[...]
    for j in range(d_inner // col_block):
        cols = slice(j * col_block, (j + 1) * col_block)

        def proj(k, lo=j * col_block):
            return jnp.dot(h, win_ref[:, k * d_inner + lo:k * d_inner + lo + col_block],
                           preferred_element_type=F32)

        xc = proj(1) * proj(2)
        xc_ref[XC_HEAD - SUBLANES:XC_HEAD, :] = carry_ref[:, cols]
        xc_ref[XC_HEAD:XC_HEAD + tm, :] = xc
        carry_ref[:, cols] = xc[tm - SUBLANES:tm]
        conv = wc_ref[CONV_W - 1:CONV_W, cols] * xc
        for k in range(CONV_W - 1):
            shift = CONV_W - 1 - k
            conv = conv + (wc_ref[k:k + 1, cols]
                           * xc_ref[XC_HEAD - shift:XC_HEAD - shift + tm, :])
        y = proj(0) * conv * _silu(proj(3))
        y_ref[:, cols] = y.astype(BF16)
    out = x + jnp.dot(y_ref[...], wout_ref[...], preferred_element_type=F32)
    if r_out is not None:
        r_out[...] = _rms_scale(out)
    if final_norm:
        out = out * _rms_scale(out) * fg_ref[...]
    o_ref[...] = out


IN_HBM = "hbm"


def _resident(a, layer=None):
    del a
    return layer


def _slab_rows(w, n_steps):
    rows = w.shape[-2] // n_steps
    assert rows * n_steps == w.shape[-2] and rows % BF16_ROWS == 0
    return rows


def _call_layer(kern, x2, scale, args, in_specs, next_weights, scratch_shapes, *, tm, name):
    n_tok, d_model = x2.shape
    n_steps = n_tok // tm
    row_tile = pl.BlockSpec((tm, d_model), lambda i: (i, 0))
    row_scale = pl.BlockSpec((tm, 1), lambda i: (i, 0))
    slabs_in = [pl.BlockSpec((None, _slab_rows(w, n_steps), w.shape[2]),
                             lambda i, layer=layer: (layer, i, 0)) for w, layer in next_weights]
    slabs_out = [pl.BlockSpec((_slab_rows(w, n_steps), w.shape[2]), lambda i: (i, 0))
                 for w, _ in next_weights]
    has_scale = scale is not None
    emit_scale = bool(next_weights)
    n_fixed, n_next = len(args), len(next_weights)
    n_tiled_in = 1 + has_scale
    tiled_in_specs = [row_tile] + [row_scale] * has_scale + slabs_in
    tiled_out_specs = [row_tile] + [row_scale] * emit_scale + slabs_out
    step_kernel = functools.partial(kern, tm=tm, n_next=n_next, has_scale=has_scale)

    def whole_call(*refs):
        n_in = n_tiled_in + n_fixed + n_next
        n_out = 1 + emit_scale + n_next
        ins, outs_, scratch = refs[:n_in], refs[n_in:n_in + n_out], refs[n_in + n_out:]
        fixed = [r if place in (None, IN_HBM) else r.at[place]
                 for r, place in zip(ins[n_tiled_in:n_tiled_in + n_fixed], in_specs)]
        streamed_in = ins[:n_tiled_in] + ins[n_tiled_in + n_fixed:]

        def step(*blocks):
            blocks_in, blocks_out = blocks[:len(streamed_in)], blocks[len(streamed_in):]
            step_kernel(*blocks_in[:n_tiled_in], *fixed, *blocks_in[n_tiled_in:],
                        *blocks_out, *scratch)

        pltpu.emit_pipeline(step, grid=(n_steps,), in_specs=tiled_in_specs,
                            out_specs=tiled_out_specs)(*streamed_in, *outs_)

    hbm = pl.BlockSpec(memory_space=pl.ANY)
    vmem = pl.BlockSpec(memory_space=pltpu.VMEM)
    outs = pl.pallas_call(
        whole_call,
        in_specs=[hbm] * n_tiled_in + [hbm if p == IN_HBM else vmem for p in in_specs]
                 + [hbm] * n_next,
        out_specs=[hbm] * (1 + emit_scale + n_next),
        out_shape=[jax.ShapeDtypeStruct(x2.shape, x2.dtype)]
                  + [jax.ShapeDtypeStruct((n_tok, 1), F32)] * emit_scale
                  + [jax.ShapeDtypeStruct(w.shape[1:], BF16) for w, _ in next_weights],
        scratch_shapes=scratch_shapes,
        compiler_params=pltpu.CompilerParams(vmem_limit_bytes=VMEM_LIMIT_BYTES),
        name=name,
    )(x2, *([scale] * has_scale), *args, *[w for w, _ in next_weights])
    if emit_scale:
        return outs[0], outs[1], outs[2:]
    return outs[0], None, outs[1:]


def _layer_a(x2, scale, norm_g, v_g, w_s, b_s, w_in, w_out, next_weights, *,
             tm, layer, sub, own_weights):
    d_model = x2.shape[1]
    d_inner = w_out.shape[-2]
    kern = functools.partial(_layer_a_kernel, d_inner=d_inner, groups=w_s.shape[1],
                             layer=layer, sub=sub, own_weights=own_weights)
    scratch = [pltpu.VMEM((tm, d_model), BF16),
               pltpu.VMEM((tm, d_inner), BF16),
               pltpu.VMEM((tm, d_inner), BF16),
               pltpu.VMEM(w_s.shape[1:], BF16),
               pltpu.VMEM((CHUNK, d_inner), F32)]
    if own_weights:
        w_specs = [IN_HBM] * 2
        scratch += [pltpu.VMEM(w_in.shape[1:], BF16),
                    pltpu.VMEM(w_out.shape[1:], BF16),
                    pltpu.VMEM(STAGE_SHAPE, F32),
                    pltpu.SemaphoreType.DMA(STAGE_SHAPE[:1])]
    else:
        w_specs = [_resident(w_in), _resident(w_out)]
    return _call_layer(
        kern, x2, scale,
        [norm_g, v_g, w_s, b_s, w_in, w_out],
        [_resident(norm_g), _resident(v_g), _resident(w_s, sub), _resident(b_s, sub)] + w_specs,
        next_weights, scratch, tm=tm, name="gmlp_layer")


def _layer_b(x2, scale, norm_g, w_conv, final_g, w_in, w_out, next_weights, *,
             tm, seq, col_block, layer, sub, final_norm):
    d_model = x2.shape[1]
    d_inner = w_out.shape[0]
    kern = functools.partial(_layer_b_kernel, d_inner=d_inner, col_block=col_block,
                             tiles_per_seq=seq // tm, layer=layer, final_norm=final_norm)
    return _call_layer(
        kern, x2, scale,
        [norm_g, w_conv, final_g, w_in, w_out],
        [_resident(norm_g), _resident(w_conv, sub), _resident(final_g),
         _resident(w_in), _resident(w_out)],
        next_weights,
        [pltpu.VMEM((tm, d_model), BF16),
         pltpu.VMEM((tm + XC_HEAD, col_block), F32),
         pltpu.VMEM((SUBLANES, d_inner), F32),
         pltpu.VMEM((tm, d_inner), BF16)],
        tm=tm, name="shortconv_layer")


def kernel(x, norm_g, final_g, a_w_in, a_v_norm_g, a_w_s, a_b_s, a_w_out, b_w_in, b_w_conv, b_w_out):
    batch, seq, d_model = x.shape
    depth = norm_g.shape[0]
    tm = 1024
    assert seq % tm == 0 and tm % CHUNK == 0
    assert depth % 2 == 0, "the final norm is fused into the last short-conv layer"
    x2 = x.reshape(batch * seq, d_model)
    assert a_w_s.shape[2:] == (CHUNK, CHUNK) and a_b_s.shape[2] == CHUNK
    final_g = final_g.reshape(1, d_model)
    w_in, w_out, scale = a_w_in, a_w_out, None
    for i in range(depth):
        j = i // 2
        if i % 2 == 0:
            x2, scale, (w_in, w_out) = _layer_a(
                x2, scale, norm_g, a_v_norm_g, a_w_s, a_b_s, w_in, w_out,
                [(b_w_in, j), (b_w_out, j)], tm=tm, layer=i, sub=j, own_weights=i == 0)
        else:
            last = i == depth - 1
            x2, scale, nxt = _layer_b(
                x2, scale, norm_g, b_w_conv, final_g, w_in, w_out,
                [] if last else [(a_w_in, j + 1), (a_w_out, j + 1)],
                tm=tm, seq=seq, col_block=256, layer=i, sub=j, final_norm=last)
            if not last:
                w_in, w_out = nxt
    return x2.reshape(batch, seq, d_model)
```

```python
import functools

import jax
import jax.numpy as jnp
from jax import lax
from jax.experimental import pallas as pl
from jax.experimental.pallas import tpu as pltpu

EPS = 1e-6
CHUNK = 128
CONV_W = 3
SUBLANES = 8
BF16_ROWS = 16
XC_HEAD = 16
VMEM_LIMIT_BYTES = 60 * 1024 * 1024
STAGE_SHAPE = (6, 256, 1024)

F32 = jnp.float32
BF16 = jnp.bfloat16


def _rms_scale(x):
    return lax.rsqrt(jnp.mean(x * x, axis=-1, keepdims=True) + EPS)


def _silu(z):
    return z * jax.nn.sigmoid(z)


def _round_next_weights(srcs, dsts):
    for src, dst in zip(srcs, dsts):
        dst[...] = src[...].astype(BF16)


def _fetch_and_round(w_hbm, layer, dst_ref, stage_ref, sem):
    max_slots, rows, cols = stage_ref.shape
    k, n = dst_ref.shape
    per_band = n // cols
    n_panels = (k // rows) * per_band
    assert n == per_band * cols and k % rows == 0
    n_slots = max(d for d in range(per_band, max_slots + 1, per_band) if n_panels % d == 0)
    bands_per_group = n_slots // per_band
    n_groups = n_panels // n_slots

    def copy(g, s):
        band, col = divmod(s, per_band)
        r0 = (g * bands_per_group + band) * rows
        c0 = col * cols
        dma = pltpu.make_async_copy(w_hbm.at[layer, pl.ds(r0, rows), pl.ds(c0, cols)],
                                    stage_ref.at[s], sem.at[s])
        return dma, r0, c0

    for s in range(n_slots):
        copy(0, s)[0].start(priority=s % 2)

    def body(g, carry):
        for s in range(n_slots):
            dma, r0, c0 = copy(g, s)
            dma.wait()
            dst_ref[pl.ds(pl.multiple_of(r0, rows), rows), c0:c0 + cols] = (
                stage_ref[s].astype(BF16))

            @pl.when(g + 1 < n_groups)
            def _(s=s):
                copy(g + 1, s)[0].start(priority=s % 2)
        return carry

    lax.fori_loop(0, n_groups, body, 0)


def _split_refs(rest, n_next, n_scratch):
    n_out = 1 + (1 if n_next else 0)
    assert len(rest) == 2 * n_next + n_out + n_scratch
    r_out = rest[n_next + 1] if n_next else None
    return (rest[:n_next], rest[n_next], r_out, rest[n_next + n_out:2 * n_next + n_out],
            rest[2 * n_next + n_out:])


def _prepare_mixing(ws_in, bs_in, ws_ref, bias_ref):
    groups = ws_in.shape[0]
    gw = bias_ref.shape[1] // groups
    t = lax.broadcasted_iota(jnp.int32, (CHUNK, CHUNK), 0)
    s = lax.broadcasted_iota(jnp.int32, (CHUNK, CHUNK), 1)
    for g in range(groups):
        ws_ref[g] = jnp.where(t >= s, ws_in[g], 0.0).astype(BF16)
        b_col = jnp.sum(jnp.where(t == s, bs_in[g:g + 1, :], 0.0), axis=-1, keepdims=True)
        bias_ref[:, g * gw:(g + 1) * gw] = jnp.broadcast_to(b_col, (CHUNK, gw))


def _layer_a_kernel(x_ref, *refs, tm, d_inner, groups, layer, sub, n_next, own_weights, has_scale):
    r_ref, refs = (refs[0], refs[1:]) if has_scale else (None, refs)
    (g_ref, vg_ref, ws_in, bs_in, win_in, wout_in), rest = refs[:6], refs[6:]
    nxt_src, o_ref, r_out, nxt_dst, scratch = _split_refs(rest, n_next, 9 if own_weights else 5)
    h_ref, vn_ref, y_ref, ws_ref, bias_ref = scratch[:5]
    if own_weights:
        win_ref, wout_ref, stage_ref, sem = scratch[5:]
    else:
        win_ref, wout_ref = win_in, wout_in

    @pl.when(pl.program_id(0) == 0)
    def _():
        _prepare_mixing(ws_in, bs_in, ws_ref, bias_ref)
        if own_weights:
            _fetch_and_round(win_in, sub, win_ref, stage_ref, sem)
            _fetch_and_round(wout_in, sub, wout_ref, stage_ref, sem)

    _round_next_weights(nxt_src, nxt_dst)

    x = x_ref[...]
    r = r_ref[...] if has_scale else _rms_scale(x)
    h_ref[...] = (x * r * g_ref[layer:layer + 1, :]).astype(BF16)
    h = h_ref[...]
    v = jnp.dot(h, win_ref[:, d_inner:2 * d_inner], preferred_element_type=F32)
    vn_ref[...] = (v * _rms_scale(v) * vg_ref[sub:sub + 1, :]).astype(BF16)
    gw = d_inner // groups
    for g in range(groups):
        lo = g * gw
        u = jnp.dot(h, win_ref[:, lo:lo + gw], preferred_element_type=F32)
        z = jnp.dot(h, win_ref[:, 2 * d_inner + lo:2 * d_inner + lo + gw],
                    preferred_element_type=F32)
        ws = ws_ref[g]
        for n in range(tm // CHUNK):
            r0 = n * CHUNK
            mixed = jnp.dot(ws, vn_ref[r0:r0 + CHUNK, lo:lo + gw],
                            preferred_element_type=F32) + bias_ref[:, lo:lo + gw]
            y = u[r0:r0 + CHUNK] * mixed * _silu(z[r0:r0 + CHUNK])
            y_ref[r0:r0 + CHUNK, lo:lo + gw] = y.astype(BF16)
    out = x + jnp.dot(y_ref[...], wout_ref[...], preferred_element_type=F32)
    if r_out is not None:
        r_out[...] = _rms_scale(out)
    o_ref[...] = out


def _layer_b_kernel(x_ref, *refs, tm, d_inner, col_block, tiles_per_seq, layer, final_norm, n_next,
                    has_scale):
    r_ref, refs = (refs[0], refs[1:]) if has_scale else (None, refs)
    (g_ref, wc_ref, fg_ref, win_ref, wout_ref), rest = refs[:5], refs[5:]
    nxt_src, o_ref, r_out, nxt_dst, (h_ref, xc_ref, carry_ref, y_ref) = _split_refs(rest, n_next, 4)

    @pl.when(pl.program_id(0) % tiles_per_seq == 0)
    def _():
        carry_ref[...] = jnp.zeros_like(carry_ref)

    _round_next_weights(nxt_src, nxt_dst)
    x = x_ref[...]
    r = r_ref[...] if has_scale else _rms_scale(x)
    h_ref[...] = (x * r * g_ref[layer:layer + 1, :]).astype(BF16)
    h = h_ref[...]
    for j in range(d_inner // col_block):
        cols = slice(j * col_block, (j + 1) * col_block)

        def proj(k, lo=j * col_block):
            return jnp.dot(h, win_ref[:, k * d_inner + lo:k * d_inner + lo + col_block],
                           preferred_element_type=F32)

        xc = proj(1) * proj(2)
        xc_ref[XC_HEAD - SUBLANES:XC_HEAD, :] = carry_ref[:, cols]
        xc_ref[XC_HEAD:XC_HEAD + tm, :] = xc
        carry_ref[:, cols] = xc[tm - SUBLANES:tm]
        conv = wc_ref[CONV_W - 1:CONV_W, cols] * xc
        for k in range(CONV_W - 1):
            shift = CONV_W - 1 - k
            conv = conv + (wc_ref[k:k + 1, cols]
                           * xc_ref[XC_HEAD - shift:XC_HEAD - shift + tm, :])
        y = proj(0) * conv * _silu(proj(3))
        y_ref[:, cols] = y.astype(BF16)
    out = x + jnp.dot(y_ref[...], wout_ref[...], preferred_element_type=F32)
    if r_out is not None:
        r_out[...] = _rms_scale(out)
    if final_norm:
        out = out * _rms_scale(out) * fg_ref[...]
    o_ref[...] = out


def _resident(a, layer=None):
    if layer is None:
        return pl.BlockSpec(a.shape, lambda i: (0,) * a.ndim, pipeline_mode=pl.Buffered(1))
    return pl.BlockSpec((None,) + a.shape[1:], lambda i: (layer,) + (0,) * (a.ndim - 1),
                        pipeline_mode=pl.Buffered(1))


def _slab_rows(w, n_steps):
    rows = w.shape[-2] // n_steps
    assert rows * n_steps == w.shape[-2] and rows % BF16_ROWS == 0
    return rows


def _call_layer(kern, x2, scale, args, in_specs, next_weights, scratch_shapes, *, tm, name):
    n_tok, d_model = x2.shape
    n_steps = n_tok // tm
    row_tile = pl.BlockSpec((tm, d_model), lambda i: (i, 0))
    row_scale = pl.BlockSpec((tm, 1), lambda i: (i, 0))
    slabs_in = [pl.BlockSpec((None, _slab_rows(w, n_steps), w.shape[2]),
                             lambda i, layer=layer: (layer, i, 0)) for w, layer in next_weights]
    slabs_out = [pl.BlockSpec((_slab_rows(w, n_steps), w.shape[2]), lambda i: (i, 0))
                 for w, _ in next_weights]
    has_scale = scale is not None
    emit_scale = bool(next_weights)
    outs = pl.pallas_call(
        functools.partial(kern, tm=tm, n_next=len(next_weights), has_scale=has_scale),
        grid=(n_steps,),
        in_specs=[row_tile] + [row_scale] * has_scale + in_specs + slabs_in,
        out_specs=[row_tile] + [row_scale] * emit_scale + slabs_out,
        out_shape=[jax.ShapeDtypeStruct(x2.shape, x2.dtype)]
                  + [jax.ShapeDtypeStruct((n_tok, 1), F32)] * emit_scale
                  + [jax.ShapeDtypeStruct(w.shape[1:], BF16) for w, _ in next_weights],
        scratch_shapes=scratch_shapes,
        compiler_params=pltpu.CompilerParams(dimension_semantics=("arbitrary",),
                                             vmem_limit_bytes=VMEM_LIMIT_BYTES),
        name=name,
    )(x2, *([scale] * has_scale), *args, *[w for w, _ in next_weights])
    if emit_scale:
        return outs[0], outs[1], outs[2:]
    return outs[0], None, outs[1:]


def _layer_a(x2, scale, norm_g, v_g, w_s, b_s, w_in, w_out, next_weights, *,
             tm, layer, sub, own_weights):
    d_model = x2.shape[1]
    d_inner = w_out.shape[-2]
    kern = functools.partial(_layer_a_kernel, d_inner=d_inner, groups=w_s.shape[1],
                             layer=layer, sub=sub, own_weights=own_weights)
    scratch = [pltpu.VMEM((tm, d_model), BF16),
               pltpu.VMEM((tm, d_inner), BF16),
               pltpu.VMEM((tm, d_inner), BF16),
               pltpu.VMEM(w_s.shape[1:], BF16),
               pltpu.VMEM((CHUNK, d_inner), F32)]
    if own_weights:
        w_specs = [pl.BlockSpec(memory_space=pl.ANY)] * 2
        scratch += [pltpu.VMEM(w_in.shape[1:], BF16),
                    pltpu.VMEM(w_out.shape[1:], BF16),
                    pltpu.VMEM(STAGE_SHAPE, F32),
                    pltpu.SemaphoreType.DMA(STAGE_SHAPE[:1])]
    else:
        w_specs = [_resident(w_in), _resident(w_out)]
    return _call_layer(
        kern, x2, scale,
        [norm_g, v_g, w_s, b_s, w_in, w_out],
        [_resident(norm_g), _resident(v_g), _resident(w_s, sub), _resident(b_s, sub)] + w_specs,
        next_weights, scratch, tm=tm, name="gmlp_layer")


def _layer_b(x2, scale, norm_g, w_conv, final_g, w_in, w_out, next_weights, *,
             tm, seq, col_block, layer, sub, final_norm):
    d_model = x2.shape[1]
    d_inner = w_out.shape[0]
    kern = functools.partial(_layer_b_kernel, d_inner=d_inner, col_block=col_block,
                             tiles_per_seq=seq // tm, layer=layer, final_norm=final_norm)
    return _call_layer(
        kern, x2, scale,
        [norm_g, w_conv, final_g, w_in, w_out],
        [_resident(norm_g), _resident(w_conv, sub), _resident(final_g),
         _resident(w_in), _resident(w_out)],
        next_weights,
        [pltpu.VMEM((tm, d_model), BF16),
         pltpu.VMEM((tm + XC_HEAD, col_block), F32),
         pltpu.VMEM((SUBLANES, d_inner), F32),
         pltpu.VMEM((tm, d_inner), BF16)],
        tm=tm, name="shortconv_layer")


def kernel(x, norm_g, final_g, a_w_in, a_v_norm_g, a_w_s, a_b_s, a_w_out, b_w_in, b_w_conv, b_w_out):
    batch, seq, d_model = x.shape
    depth = norm_g.shape[0]
    tm = 1024
    assert seq % tm == 0 and tm % CHUNK == 0
    assert depth % 2 == 0, "the final norm is fused into the last short-conv layer"
    x2 = x.reshape(batch * seq, d_model)
    assert a_w_s.shape[2:] == (CHUNK, CHUNK) and a_b_s.shape[2] == CHUNK
    final_g = final_g.reshape(1, d_model)
    w_in, w_out, scale = a_w_in, a_w_out, None
    for i in range(depth):
        j = i // 2
        if i % 2 == 0:
            x2, scale, (w_in, w_out) = _layer_a(
                x2, scale, norm_g, a_v_norm_g, a_w_s, a_b_s, w_in, w_out,
                [(b_w_in, j), (b_w_out, j)], tm=tm, layer=i, sub=j, own_weights=i == 0)
        else:
            last = i == depth - 1
            x2, scale, nxt = _layer_b(
                x2, scale, norm_g, b_w_conv, final_g, w_in, w_out,
                [] if last else [(a_w_in, j + 1), (a_w_out, j + 1)],
                tm=tm, seq=seq, col_block=256, layer=i, sub=j, final_norm=last)
            if not last:
                w_in, w_out = nxt
    return x2.reshape(batch, seq, d_model)
```

```python
import functools

import jax
import jax.numpy as jnp
from jax import lax
from jax.experimental import pallas as pl
from jax.experimental.pallas import tpu as pltpu

EPS = 1e-6
CHUNK = 128
CONV_W = 3
SUBLANES = 8
BF16_ROWS = 16
XC_HEAD = 16
VMEM_LIMIT_BYTES = 60 * 1024 * 1024
STAGE_SHAPE = (6, 256, 1024)

F32 = jnp.float32
BF16 = jnp.bfloat16


def _rms_scale(x):
    return lax.rsqrt(jnp.mean(x * x, axis=-1, keepdims=True) + EPS)


def _silu(z):
    return z * jax.nn.sigmoid(z)


def _round_next_weights(srcs, dsts):
    for src, dst in zip(srcs, dsts):
        dst[...] = src[...].astype(BF16)


def _fetch_and_round(w_hbm, layer, dst_ref, stage_ref, sem):
    max_slots, rows, cols = stage_ref.shape
    k, n = dst_ref.shape
    per_band = n // cols
    n_panels = (k // rows) * per_band
    assert n == per_band * cols and k % rows == 0
    n_slots = max(d for d in range(per_band, max_slots + 1, per_band) if n_panels % d == 0)
    bands_per_group = n_slots // per_band
    n_groups = n_panels // n_slots

    def copy(g, s):
        band, col = divmod(s, per_band)
        r0 = (g * bands_per_group + band) * rows
        c0 = col * cols
        dma = pltpu.make_async_copy(w_hbm.at[layer, pl.ds(r0, rows), pl.ds(c0, cols)],
                                    stage_ref.at[s], sem.at[s])
        return dma, r0, c0

    for s in range(n_slots):
        copy(0, s)[0].start()

    def body(g, carry):
        for s in range(n_slots):
            dma, r0, c0 = copy(g, s)
            dma.wait()
            dst_ref[pl.ds(pl.multiple_of(r0, rows), rows), c0:c0 + cols] = (
                stage_ref[s].astype(BF16))

            @pl.when(g + 1 < n_groups)
            def _():
                copy(g + 1, s)[0].start()
        return carry

    lax.fori_loop(0, n_groups, body, 0)


def _split_refs(rest, n_next, n_scratch):
    n_out = 1 + (1 if n_next else 0)
    assert len(rest) == 2 * n_next + n_out + n_scratch
    r_out = rest[n_next + 1] if n_next else None
    return (rest[:n_next], rest[n_next], r_out, rest[n_next + n_out:2 * n_next + n_out],
            rest[2 * n_next + n_out:])


def _prepare_mixing(ws_in, bs_in, ws_ref, bias_ref):
    groups = ws_in.shape[0]
    gw = bias_ref.shape[1] // groups
    t = lax.broadcasted_iota(jnp.int32, (CHUNK, CHUNK), 0)
    s = lax.broadcasted_iota(jnp.int32, (CHUNK, CHUNK), 1)
    for g in range(groups):
        ws_ref[g] = jnp.where(t >= s, ws_in[g], 0.0).astype(BF16)
        b_col = jnp.sum(jnp.where(t == s, bs_in[g:g + 1, :], 0.0), axis=-1, keepdims=True)
        bias_ref[:, g * gw:(g + 1) * gw] = jnp.broadcast_to(b_col, (CHUNK, gw))


def _layer_a_kernel(x_ref, *refs, tm, d_inner, groups, layer, sub, n_next, own_weights, has_scale):
    r_ref, refs = (refs[0], refs[1:]) if has_scale else (None, refs)
    (g_ref, vg_ref, ws_in, bs_in, win_in, wout_in), rest = refs[:6], refs[6:]
    nxt_src, o_ref, r_out, nxt_dst, scratch = _split_refs(rest, n_next, 9 if own_weights else 5)
    h_ref, vn_ref, y_ref, ws_ref, bias_ref = scratch[:5]
    if own_weights:
        win_ref, wout_ref, stage_ref, sem = scratch[5:]
    else:
        win_ref, wout_ref = win_in, wout_in

    @pl.when(pl.program_id(0) == 0)
    def _():
        _prepare_mixing(ws_in, bs_in, ws_ref, bias_ref)
        if own_weights:
            _fetch_and_round(win_in, sub, win_ref, stage_ref, sem)
            _fetch_and_round(wout_in, sub, wout_ref, stage_ref, sem)

    _round_next_weights(nxt_src, nxt_dst)

    x = x_ref[...]
    r = r_ref[...] if has_scale else _rms_scale(x)
    h_ref[...] = (x * r * g_ref[layer:layer + 1, :]).astype(BF16)
    h = h_ref[...]
    v = jnp.dot(h, win_ref[:, d_inner:2 * d_inner], preferred_element_type=F32)
    vn_ref[...] = (v * _rms_scale(v) * vg_ref[sub:sub + 1, :]).astype(BF16)
    gw = d_inner // groups
    for g in range(groups):
        lo = g * gw
        u = jnp.dot(h, win_ref[:, lo:lo + gw], preferred_element_type=F32)
        z = jnp.dot(h, win_ref[:, 2 * d_inner + lo:2 * d_inner + lo + gw],
                    preferred_element_type=F32)
        ws = ws_ref[g]
        for n in range(tm // CHUNK):
            r0 = n * CHUNK
            mixed = jnp.dot(ws, vn_ref[r0:r0 + CHUNK, lo:lo + gw],
                            preferred_element_type=F32) + bias_ref[:, lo:lo + gw]
            y = u[r0:r0 + CHUNK] * mixed * _silu(z[r0:r0 + CHUNK])
            y_ref[r0:r0 + CHUNK, lo:lo + gw] = y.astype(BF16)
    out = x + jnp.dot(y_ref[...], wout_ref[...], preferred_element_type=F32)
    if r_out is not None:
        r_out[...] = _rms_scale(out)
    o_ref[...] = out


def _layer_b_kernel(x_ref, *refs, tm, d_inner, col_block, tiles_per_seq, layer, final_norm, n_next,
                    has_scale):
    r_ref, refs = (refs[0], refs[1:]) if has_scale else (None, refs)
    (g_ref, wc_ref, fg_ref, win_ref, wout_ref), rest = refs[:5], refs[5:]
    nxt_src, o_ref, r_out, nxt_dst, (h_ref, xc_ref, carry_ref, y_ref) = _split_refs(rest, n_next, 4)

    @pl.when(pl.program_id(0) % tiles_per_seq == 0)
    def _():
        carry_ref[...] = jnp.zeros_like(carry_ref)

    _round_next_weights(nxt_src, nxt_dst)
    x = x_ref[...]
    r = r_ref[...] if has_scale else _rms_scale(x)
    h_ref[...] = (x * r * g_ref[layer:layer + 1, :]).astype(BF16)
    h = h_ref[...]
    for j in range(d_inner // col_block):
        cols = slice(j * col_block, (j + 1) * col_block)

        def proj(k, lo=j * col_block):
            return jnp.dot(h, win_ref[:, k * d_inner + lo:k * d_inner + lo + col_block],
                           preferred_element_type=F32)

        xc = proj(1) * proj(2)
        prev = carry_ref[:, cols]
        carry_ref[:, cols] = xc[tm - SUBLANES:tm]
        head_row = lax.broadcasted_iota(jnp.int32, (SUBLANES, col_block), 0)
        conv = wc_ref[CONV_W - 1:CONV_W, cols] * xc
        for k in range(CONV_W - 1):
            shift = CONV_W - 1 - k
            rolled = pltpu.roll(xc, shift, axis=0)
            head = jnp.where(head_row < shift, pltpu.roll(prev, shift, axis=0), rolled[:SUBLANES])
            shifted = jnp.concatenate([head, rolled[SUBLANES:]], axis=0)
            conv = conv + wc_ref[k:k + 1, cols] * shifted
        y = proj(0) * conv * _silu(proj(3))
        y_ref[:, cols] = y.astype(BF16)
    out = x + jnp.dot(y_ref[...], wout_ref[...], preferred_element_type=F32)
    if r_out is not None:
        r_out[...] = _rms_scale(out)
    if final_norm:
        out = out * _rms_scale(out) * fg_ref[...]
    o_ref[...] = out


def _resident(a, layer=None):
    if layer is None:
        return pl.BlockSpec(a.shape, lambda i: (0,) * a.ndim, pipeline_mode=pl.Buffered(1))
    return pl.BlockSpec((None,) + a.shape[1:], lambda i: (layer,) + (0,) * (a.ndim - 1),
                        pipeline_mode=pl.Buffered(1))


def _slab_rows(w, n_steps):
    rows = w.shape[-2] // n_steps
    assert rows * n_steps == w.shape[-2] and rows % BF16_ROWS == 0
    return rows


def _call_layer(kern, x2, scale, args, in_specs, next_weights, scratch_shapes, *, tm, name):
    n_tok, d_model = x2.shape
    n_steps = n_tok // tm
    row_tile = pl.BlockSpec((tm, d_model), lambda i: (i, 0))
    row_scale = pl.BlockSpec((tm, 1), lambda i: (i, 0))
    slabs_in = [pl.BlockSpec((None, _slab_rows(w, n_steps), w.shape[2]),
                             lambda i, layer=layer: (layer, i, 0)) for w, layer in next_weights]
    slabs_out = [pl.BlockSpec((_slab_rows(w, n_steps), w.shape[2]), lambda i: (i, 0))
                 for w, _ in next_weights]
    has_scale = scale is not None
    emit_scale = bool(next_weights)
    outs = pl.pallas_call(
        functools.partial(kern, tm=tm, n_next=len(next_weights), has_scale=has_scale),
        grid=(n_steps,),
        in_specs=[row_tile] + [row_scale] * has_scale + in_specs + slabs_in,
        out_specs=[row_tile] + [row_scale] * emit_scale + slabs_out,
        out_shape=[jax.ShapeDtypeStruct(x2.shape, x2.dtype)]
                  + [jax.ShapeDtypeStruct((n_tok, 1), F32)] * emit_scale
                  + [jax.ShapeDtypeStruct(w.shape[1:], BF16) for w, _ in next_weights],
        scratch_shapes=scratch_shapes,
        compiler_params=pltpu.CompilerParams(dimension_semantics=("arbitrary",),
                                             vmem_limit_bytes=VMEM_LIMIT_BYTES),
        name=name,
    )(x2, *([scale] * has_scale), *args, *[w for w, _ in next_weights])
    if emit_scale:
        return outs[0], outs[1], outs[2:]
    return outs[0], None, outs[1:]


def _layer_a(x2, scale, norm_g, v_g, w_s, b_s, w_in, w_out, next_weights, *,
             tm, layer, sub, own_weights):
    d_model = x2.shape[1]
    d_inner = w_out.shape[-2]
    kern = functools.partial(_layer_a_kernel, d_inner=d_inner, groups=w_s.shape[1],
                             layer=layer, sub=sub, own_weights=own_weights)
    scratch = [pltpu.VMEM((tm, d_model), BF16),
               pltpu.VMEM((tm, d_inner), BF16),
               pltpu.VMEM((tm, d_inner), BF16),
               pltpu.VMEM(w_s.shape[1:], BF16),
               pltpu.VMEM((CHUNK, d_inner), F32)]
    if own_weights:
        w_specs = [pl.BlockSpec(memory_space=pl.ANY)] * 2
        scratch += [pltpu.VMEM(w_in.shape[1:], BF16),
                    pltpu.VMEM(w_out.shape[1:], BF16),
                    pltpu.VMEM(STAGE_SHAPE, F32),
                    pltpu.SemaphoreType.DMA(STAGE_SHAPE[:1])]
    else:
        w_specs = [_resident(w_in), _resident(w_out)]
    return _call_layer(
        kern, x2, scale,
        [norm_g, v_g, w_s, b_s, w_in, w_out],
        [_resident(norm_g), _resident(v_g), _resident(w_s, sub), _resident(b_s, sub)] + w_specs,
        next_weights, scratch, tm=tm, name="gmlp_layer")


def _layer_b(x2, scale, norm_g, w_conv, final_g, w_in, w_out, next_weights, *,
             tm, seq, col_block, layer, sub, final_norm):
    d_model = x2.shape[1]
    d_inner = w_out.shape[0]
    kern = functools.partial(_layer_b_kernel, d_inner=d_inner, col_block=col_block,
                             tiles_per_seq=seq // tm, layer=layer, final_norm=final_norm)
    return _call_layer(
        kern, x2, scale,
        [norm_g, w_conv, final_g, w_in, w_out],
        [_resident(norm_g), _resident(w_conv, sub), _resident(final_g),
         _resident(w_in), _resident(w_out)],
        next_weights,
        [pltpu.VMEM((tm, d_model), BF16),
         pltpu.VMEM((tm + XC_HEAD, col_block), F32),
         pltpu.VMEM((SUBLANES, d_inner), F32),
         pltpu.VMEM((tm, d_inner), BF16)],
        tm=tm, name="shortconv_layer")


def kernel(x, norm_g, final_g, a_w_in, a_v_norm_g, a_w_s, a_b_s, a_w_out, b_w_in, b_w_conv, b_w_out):
    batch, seq, d_model = x.shape
    depth = norm_g.shape[0]
    tm = 1024
    assert seq % tm == 0 and tm % CHUNK == 0
    assert depth % 2 == 0, "the final norm is fused into the last short-conv layer"
    x2 = x.reshape(batch * seq, d_model)
    assert a_w_s.shape[2:] == (CHUNK, CHUNK) and a_b_s.shape[2] == CHUNK
    final_g = final_g.reshape(1, d_model)
    w_in, w_out, scale = a_w_in, a_w_out, None
    for i in range(depth):
        j = i // 2
        if i % 2 == 0:
            x2, scale, (w_in, w_out) = _layer_a(
                x2, scale, norm_g, a_v_norm_g, a_w_s, a_b_s, w_in, w_out,
                [(b_w_in, j), (b_w_out, j)], tm=tm, layer=i, sub=j, own_weights=i == 0)
        else:
            last = i == depth - 1
            x2, scale, nxt = _layer_b(
                x2, scale, norm_g, b_w_conv, final_g, w_in, w_out,
                [] if last else [(a_w_in, j + 1), (a_w_out, j + 1)],
                tm=tm, seq=seq, col_block=256, layer=i, sub=j, final_norm=last)
            if not last:
                w_in, w_out = nxt
    return x2.reshape(batch, seq, d_model)
```

```python
import functools

import jax
import jax.numpy as jnp
from jax import lax
from jax.experimental import pallas as pl
from jax.experimental.pallas import tpu as pltpu

EPS = 1e-6
CHUNK = 128
CONV_W = 3
SUBLANES = 8
BF16_ROWS = 16
XC_HEAD = 16
VMEM_LIMIT_BYTES = 60 * 1024 * 1024
STAGE_SHAPE = (6, 256, 1024)

F32 = jnp.float32
BF16 = jnp.bfloat16


def _rms_scale(x):
    return lax.rsqrt(jnp.mean(x * x, axis=-1, keepdims=True) + EPS)


def _silu(z):
    return z * (0.5 * jnp.tanh(0.5 * z) + 0.5)


def _round_next_weights(srcs, dsts):
    for src, dst in zip(srcs, dsts):
        dst[...] = src[...].astype(BF16)


def _fetch_and_round(w_hbm, layer, dst_ref, stage_ref, sem):
    max_slots, rows, cols = stage_ref.shape
    k, n = dst_ref.shape
    per_band = n // cols
    n_panels = (k // rows) * per_band
    assert n == per_band * cols and k % rows == 0
    n_slots = max(d for d in range(per_band, max_slots + 1, per_band) if n_panels % d == 0)
    bands_per_group = n_slots // per_band
    n_groups = n_panels // n_slots

    def copy(g, s):
        band, col = divmod(s, per_band)
        r0 = (g * bands_per_group + band) * rows
        c0 = col * cols
        dma = pltpu.make_async_copy(w_hbm.at[layer, pl.ds(r0, rows), pl.ds(c0, cols)],
                                    stage_ref.at[s], sem.at[s])
        return dma, r0, c0

    for s in range(n_slots):
        copy(0, s)[0].start()

    def body(g, carry):
        for s in range(n_slots):
            dma, r0, c0 = copy(g, s)
            dma.wait()
            dst_ref[pl.ds(pl.multiple_of(r0, rows), rows), c0:c0 + cols] = (
                stage_ref[s].astype(BF16))

            @pl.when(g + 1 < n_groups)
            def _():
                copy(g + 1, s)[0].start()
        return carry

    lax.fori_loop(0, n_groups, body, 0)


def _split_refs(rest, n_next, n_scratch):
    n_out = 1 + (1 if n_next else 0)
    assert len(rest) == 2 * n_next + n_out + n_scratch
    r_out = rest[n_next + 1] if n_next else None
    return (rest[:n_next], rest[n_next], r_out, rest[n_next + n_out:2 * n_next + n_out],
            rest[2 * n_next + n_out:])


def _prepare_mixing(ws_in, bs_in, ws_ref, bias_ref):
    groups = ws_in.shape[0]
    gw = bias_ref.shape[1] // groups
    t = lax.broadcasted_iota(jnp.int32, (CHUNK, CHUNK), 0)
    s = lax.broadcasted_iota(jnp.int32, (CHUNK, CHUNK), 1)
    for g in range(groups):
        ws_ref[g] = jnp.where(t >= s, ws_in[g], 0.0).astype(BF16)
        b_col = jnp.sum(jnp.where(t == s, bs_in[g:g + 1, :], 0.0), axis=-1, keepdims=True)
        bias_ref[:, g * gw:(g + 1) * gw] = jnp.broadcast_to(b_col, (CHUNK, gw))


def _layer_a_kernel(x_ref, *refs, tm, d_inner, groups, layer, sub, n_next, own_weights, has_scale):
    r_ref, refs = (refs[0], refs[1:]) if has_scale else (None, refs)
    (g_ref, vg_ref, ws_in, bs_in, win_in, wout_in), rest = refs[:6], refs[6:]
    nxt_src, o_ref, r_out, nxt_dst, scratch = _split_refs(rest, n_next, 9 if own_weights else 5)
    h_ref, vn_ref, y_ref, ws_ref, bias_ref = scratch[:5]
    if own_weights:
        win_ref, wout_ref, stage_ref, sem = scratch[5:]
    else:
        win_ref, wout_ref = win_in, wout_in

    @pl.when(pl.program_id(0) == 0)
    def _():
        _prepare_mixing(ws_in, bs_in, ws_ref, bias_ref)
        if own_weights:
            _fetch_and_round(win_in, sub, win_ref, stage_ref, sem)
            _fetch_and_round(wout_in, sub, wout_ref, stage_ref, sem)

    _round_next_weights(nxt_src, nxt_dst)

    x = x_ref[...]
    r = r_ref[...] if has_scale else _rms_scale(x)
    h_ref[...] = (x * r * g_ref[layer:layer + 1, :]).astype(BF16)
    h = h_ref[...]
    v = jnp.dot(h, win_ref[:, d_inner:2 * d_inner], preferred_element_type=F32)
    vn_ref[...] = (v * _rms_scale(v) * vg_ref[sub:sub + 1, :]).astype(BF16)
    gw = d_inner // groups
    for g in range(groups):
        lo = g * gw
        u = jnp.dot(h, win_ref[:, lo:lo + gw], preferred_element_type=F32)
        z = jnp.dot(h, win_ref[:, 2 * d_inner + lo:2 * d_inner + lo + gw],
                    preferred_element_type=F32)
        ws = ws_ref[g]
        for n in range(tm // CHUNK):
            r0 = n * CHUNK
            mixed = jnp.dot(ws, vn_ref[r0:r0 + CHUNK, lo:lo + gw],
                            preferred_element_type=F32) + bias_ref[:, lo:lo + gw]
            y = u[r0:r0 + CHUNK] * mixed * _silu(z[r0:r0 + CHUNK])
            y_ref[r0:r0 + CHUNK, lo:lo + gw] = y.astype(BF16)
    out = x + jnp.dot(y_ref[...], wout_ref[...], preferred_element_type=F32)
    if r_out is not None:
        r_out[...] = _rms_scale(out)
    o_ref[...] = out


def _layer_b_kernel(x_ref, *refs, tm, d_inner, col_block, tiles_per_seq, layer, final_norm, n_next,
                    has_scale):
    r_ref, refs = (refs[0], refs[1:]) if has_scale else (None, refs)
    (g_ref, wc_ref, fg_ref, win_ref, wout_ref), rest = refs[:5], refs[5:]
    nxt_src, o_ref, r_out, nxt_dst, (h_ref, xc_ref, carry_ref, y_ref) = _split_refs(rest, n_next, 4)

    @pl.when(pl.program_id(0) % tiles_per_seq == 0)
    def _():
        carry_ref[...] = jnp.zeros_like(carry_ref)

    _round_next_weights(nxt_src, nxt_dst)
    x = x_ref[...]
    r = r_ref[...] if has_scale else _rms_scale(x)
    h_ref[...] = (x * r * g_ref[layer:layer + 1, :]).astype(BF16)
    h = h_ref[...]
    for j in range(d_inner // col_block):
        cols = slice(j * col_block, (j + 1) * col_block)

        def proj(k, lo=j * col_block):
            return jnp.dot(h, win_ref[:, k * d_inner + lo:k * d_inner + lo + col_block],
                           preferred_element_type=F32)

        xc = proj(1) * proj(2)
        xc_ref[XC_HEAD - SUBLANES:XC_HEAD, :] = carry_ref[:, cols]
        xc_ref[XC_HEAD:XC_HEAD + tm, :] = xc
        carry_ref[:, cols] = xc[tm - SUBLANES:tm]
        conv = wc_ref[CONV_W - 1:CONV_W, cols] * xc
        for k in range(CONV_W - 1):
            shift = CONV_W - 1 - k
            conv = conv + (wc_ref[k:k + 1, cols]
                           * xc_ref[XC_HEAD - shift:XC_HEAD - shift + tm, :])
        y = proj(0) * conv * _silu(proj(3))
        y_ref[:, cols] = y.astype(BF16)
    out = x + jnp.dot(y_ref[...], wout_ref[...], preferred_element_type=F32)
    if r_out is not None:
        r_out[...] = _rms_scale(out)
    if final_norm:
        out = out * _rms_scale(out) * fg_ref[...]
    o_ref[...] = out


def _resident(a, layer=None):
    if layer is None:
        return pl.BlockSpec(a.shape, lambda i: (0,) * a.ndim, pipeline_mode=pl.Buffered(1))
    return pl.BlockSpec((None,) + a.shape[1:], lambda i: (layer,) + (0,) * (a.ndim - 1),
                        pipeline_mode=pl.Buffered(1))


def _slab_rows(w, n_steps):
    rows = w.shape[-2] // n_steps
    assert rows * n_steps == w.shape[-2] and rows % BF16_ROWS == 0
    return rows


def _call_layer(kern, x2, scale, args, in_specs, next_weights, scratch_shapes, *, tm, name):
    n_tok, d_model = x2.shape
    n_steps = n_tok // tm
    row_tile = pl.BlockSpec((tm, d_model), lambda i: (i, 0))
    row_scale = pl.BlockSpec((tm, 1), lambda i: (i, 0))
    slabs_in = [pl.BlockSpec((None, _slab_rows(w, n_steps), w.shape[2]),
                             lambda i, layer=layer: (layer, i, 0)) for w, layer in next_weights]
    slabs_out = [pl.BlockSpec((_slab_rows(w, n_steps), w.shape[2]), lambda i: (i, 0))
                 for w, _ in next_weights]
    has_scale = scale is not None
    emit_scale = bool(next_weights)
    outs = pl.pallas_call(
        functools.partial(kern, tm=tm, n_next=len(next_weights), has_scale=has_scale),
        grid=(n_steps,),
        in_specs=[row_tile] + [row_scale] * has_scale + in_specs + slabs_in,
        out_specs=[row_tile] + [row_scale] * emit_scale + slabs_out,
        out_shape=[jax.ShapeDtypeStruct(x2.shape, x2.dtype)]
                  + [jax.ShapeDtypeStruct((n_tok, 1), F32)] * emit_scale
                  + [jax.ShapeDtypeStruct(w.shape[1:], BF16) for w, _ in next_weights],
        scratch_shapes=scratch_shapes,
        compiler_params=pltpu.CompilerParams(dimension_semantics=("arbitrary",),
                                             vmem_limit_bytes=VMEM_LIMIT_BYTES),
        name=name,
    )(x2, *([scale] * has_scale), *args, *[w for w, _ in next_weights])
    if emit_scale:
        return outs[0], outs[1], outs[2:]
    return outs[0], None, outs[1:]


def _layer_a(x2, scale, norm_g, v_g, w_s, b_s, w_in, w_out, next_weights, *,
             tm, layer, sub, own_weights):
    d_model = x2.shape[1]
    d_inner = w_out.shape[-2]
    kern = functools.partial(_layer_a_kernel, d_inner=d_inner, groups=w_s.shape[1],
                             layer=layer, sub=sub, own_weights=own_weights)
    scratch = [pltpu.VMEM((tm, d_model), BF16),
               pltpu.VMEM((tm, d_inner), BF16),
               pltpu.VMEM((tm, d_inner), BF16),
               pltpu.VMEM(w_s.shape[1:], BF16),
               pltpu.VMEM((CHUNK, d_inner), F32)]
    if own_weights:
        w_specs = [pl.BlockSpec(memory_space=pl.ANY)] * 2
        scratch += [pltpu.VMEM(w_in.shape[1:], BF16),
                    pltpu.VMEM(w_out.shape[1:], BF16),
                    pltpu.VMEM(STAGE_SHAPE, F32),
                    pltpu.SemaphoreType.DMA(STAGE_SHAPE[:1])]
    else:
        w_specs = [_resident(w_in), _resident(w_out)]
    return _call_layer(
        kern, x2, scale,
        [norm_g, v_g, w_s, b_s, w_in, w_out],
        [_resident(norm_g), _resident(v_g), _resident(w_s, sub), _resident(b_s, sub)] + w_specs,
        next_weights, scratch, tm=tm, name="gmlp_layer")


def _layer_b(x2, scale, norm_g, w_conv, final_g, w_in, w_out, next_weights, *,
             tm, seq, col_block, layer, sub, final_norm):
    d_model = x2.shape[1]
    d_inner = w_out.shape[0]
    kern = functools.partial(_layer_b_kernel, d_inner=d_inner, col_block=col_block,
                             tiles_per_seq=seq // tm, layer=layer, final_norm=final_norm)
    return _call_layer(
        kern, x2, scale,
        [norm_g, w_conv, final_g, w_in, w_out],
        [_resident(norm_g), _resident(w_conv, sub), _resident(final_g),
         _resident(w_in), _resident(w_out)],
        next_weights,
        [pltpu.VMEM((tm, d_model), BF16),
         pltpu.VMEM((tm + XC_HEAD, col_block), F32),
         pltpu.VMEM((SUBLANES, d_inner), F32),
         pltpu.VMEM((tm, d_inner), BF16)],
        tm=tm, name="shortconv_layer")


def kernel(x, norm_g, final_g, a_w_in, a_v_norm_g, a_w_s, a_b_s, a_w_out, b_w_in, b_w_conv, b_w_out):
    batch, seq, d_model = x.shape
    depth = norm_g.shape[0]
    tm = 1024
    assert seq % tm == 0 and tm % CHUNK == 0
    assert depth % 2 == 0, "the final norm is fused into the last short-conv layer"
    x2 = x.reshape(batch * seq, d_model)
    assert a_w_s.shape[2:] == (CHUNK, CHUNK) and a_b_s.shape[2] == CHUNK
    final_g = final_g.reshape(1, d_model)
    w_in, w_out, scale = a_w_in, a_w_out, None
    for i in range(depth):
        j = i // 2
        if i % 2 == 0:
            x2, scale, (w_in, w_out) = _layer_a(
                x2, scale, norm_g, a_v_norm_g, a_w_s, a_b_s, w_in, w_out,
                [(b_w_in, j), (b_w_out, j)], tm=tm, layer=i, sub=j, own_weights=i == 0)
        else:
            last = i == depth - 1
            x2, scale, nxt = _layer_b(
                x2, scale, norm_g, b_w_conv, final_g, w_in, w_out,
                [] if last else [(a_w_in, j + 1), (a_w_out, j + 1)],
                tm=tm, seq=seq, col_block=256, layer=i, sub=j, final_norm=last)
            if not last:
                w_in, w_out = nxt
    return x2.reshape(batch, seq, d_model)
```

```python
import functools

import jax
import jax.numpy as jnp
from jax import lax
from jax.experimental import pallas as pl
from jax.experimental.pallas import tpu as pltpu

EPS = 1e-6
CHUNK = 128
CONV_W = 3
SUBLANES = 8
BF16_ROWS = 16
XC_HEAD = 16
VMEM_LIMIT_BYTES = 60 * 1024 * 1024
STAGE_SHAPE = (6, 256, 1024)

F32 = jnp.float32
BF16 = jnp.bfloat16


def _rms_scale(x):
    return lax.rsqrt(jnp.mean(x * x, axis=-1, keepdims=True) + EPS)


def _silu(z):
    return z * (0.5 * jnp.tanh(0.5 * z) + 0.5)


def _round_next_weights(srcs, dsts):
    for src, dst in zip(srcs, dsts):
        dst[...] = src[...].astype(BF16)


def _fetch_and_round(w_hbm, layer, dst_ref, stage_ref, sem):
    max_slots, rows, cols = stage_ref.shape
    k, n = dst_ref.shape
    per_band = n // cols
    n_panels = (k // rows) * per_band
    assert n == per_band * cols and k % rows == 0
    n_slots = max(d for d in range(per_band, max_slots + 1, per_band) if n_panels % d == 0)
    bands_per_group = n_slots // per_band
    n_groups = n_panels // n_slots

    def copy(g, s):
        band, col = divmod(s, per_band)
        r0 = (g * bands_per_group + band) * rows
        c0 = col * cols
        dma = pltpu.make_async_copy(w_hbm.at[layer, pl.ds(r0, rows), pl.ds(c0, cols)],
                                    stage_ref.at[s], sem.at[s])
        return dma, r0, c0

    for s in range(n_slots):
        copy(0, s)[0].start()

    def body(g, carry):
        for s in range(n_slots):
            dma, r0, c0 = copy(g, s)
            dma.wait()
            dst_ref[pl.ds(pl.multiple_of(r0, rows), rows), c0:c0 + cols] = (
                stage_ref[s].astype(BF16))

            @pl.when(g + 1 < n_groups)
            def _():
                copy(g + 1, s)[0].start()
        return carry

    lax.fori_loop(0, n_groups, body, 0)


def _split_refs(rest, n_next, n_scratch):
    n_out = 1 + (1 if n_next else 0)
    assert len(rest) == 2 * n_next + n_out + n_scratch
    r_out = rest[n_next + 1] if n_next else None
    return (rest[:n_next], rest[n_next], r_out, rest[n_next + n_out:2 * n_next + n_out],
            rest[2 * n_next + n_out:])


def _prepare_mixing(ws_in, bs_in, ws_ref, bias_ref):
    groups = ws_in.shape[0]
    gw = bias_ref.shape[1] // groups
    t = lax.broadcasted_iota(jnp.int32, (CHUNK, CHUNK), 0)
    s = lax.broadcasted_iota(jnp.int32, (CHUNK, CHUNK), 1)
    for g in range(groups):
        ws_ref[g] = jnp.where(t >= s, ws_in[g], 0.0).astype(BF16)
        b_col = jnp.sum(jnp.where(t == s, bs_in[g:g + 1, :], 0.0), axis=-1, keepdims=True)
        bias_ref[:, g * gw:(g + 1) * gw] = jnp.broadcast_to(b_col, (CHUNK, gw))


def _layer_a_kernel(x_ref, *refs, tm, d_inner, groups, layer, sub, n_next, own_weights, has_scale):
    r_ref, refs = (refs[0], refs[1:]) if has_scale else (None, refs)
    (g_ref, vg_ref, ws_in, bs_in, win_in, wout_in), rest = refs[:6], refs[6:]
    nxt_src, o_ref, r_out, nxt_dst, scratch = _split_refs(rest, n_next, 9 if own_weights else 5)
    h_ref, vn_ref, y_ref, ws_ref, bias_ref = scratch[:5]
    if own_weights:
        win_ref, wout_ref, stage_ref, sem = scratch[5:]
    else:
        win_ref, wout_ref = win_in, wout_in

    @pl.when(pl.program_id(0) == 0)
    def _():
        _prepare_mixing(ws_in, bs_in, ws_ref, bias_ref)
        if own_weights:
            _fetch_and_round(win_in, sub, win_ref, stage_ref, sem)
            _fetch_and_round(wout_in, sub, wout_ref, stage_ref, sem)

    _round_next_weights(nxt_src, nxt_dst)

    x = x_ref[...]
    r = r_ref[...] if has_scale else _rms_scale(x)
    h_ref[...] = (x * r * g_ref[layer:layer + 1, :]).astype(BF16)
    h = h_ref[...]
    for r0 in range(0, tm, tm if has_scale else tm // 2):
        rows = slice(r0, r0 + (tm if has_scale else tm // 2))
        v = jnp.dot(h[rows], win_ref[:, d_inner:2 * d_inner], preferred_element_type=F32)
        vn_ref[rows, :] = (v * _rms_scale(v) * vg_ref[sub:sub + 1, :]).astype(BF16)
    gw = d_inner // groups
    for g in range(groups):
        lo = g * gw
        u = jnp.dot(h, win_ref[:, lo:lo + gw], preferred_element_type=F32)
        z = jnp.dot(h, win_ref[:, 2 * d_inner + lo:2 * d_inner + lo + gw],
                    preferred_element_type=F32)
        ws = ws_ref[g]
        for n in range(tm // CHUNK):
            r0 = n * CHUNK
            mixed = jnp.dot(ws, vn_ref[r0:r0 + CHUNK, lo:lo + gw],
                            preferred_element_type=F32) + bias_ref[:, lo:lo + gw]
            y = u[r0:r0 + CHUNK] * mixed * _silu(z[r0:r0 + CHUNK])
            y_ref[r0:r0 + CHUNK, lo:lo + gw] = y.astype(BF16)
    out = x + jnp.dot(y_ref[...], wout_ref[...], preferred_element_type=F32)
    if r_out is not None:
        r_out[...] = _rms_scale(out)
    o_ref[...] = out


def _layer_b_kernel(x_ref, *refs, tm, d_inner, col_block, tiles_per_seq, layer, final_norm, n_next,
                    has_scale):
    r_ref, refs = (refs[0], refs[1:]) if has_scale else (None, refs)
    (g_ref, wc_ref, fg_ref, win_ref, wout_ref), rest = refs[:5], refs[5:]
    nxt_src, o_ref, r_out, nxt_dst, (h_ref, xc_ref, carry_ref, y_ref) = _split_refs(rest, n_next, 4)

    @pl.when(pl.program_id(0) % tiles_per_seq == 0)
    def _():
        carry_ref[...] = jnp.zeros_like(carry_ref)

    _round_next_weights(nxt_src, nxt_dst)
    x = x_ref[...]
    r = r_ref[...] if has_scale else _rms_scale(x)
    h_ref[...] = (x * r * g_ref[layer:layer + 1, :]).astype(BF16)
    h = h_ref[...]
    for j in range(d_inner // col_block):
        cols = slice(j * col_block, (j + 1) * col_block)

        def proj(k, lo=j * col_block):
            return jnp.dot(h, win_ref[:, k * d_inner + lo:k * d_inner + lo + col_block],
                           preferred_element_type=F32)

        xc = proj(1) * proj(2)
        xc_ref[XC_HEAD - SUBLANES:XC_HEAD, :] = carry_ref[:, cols]
        xc_ref[XC_HEAD:XC_HEAD + tm, :] = xc
        carry_ref[:, cols] = xc[tm - SUBLANES:tm]
        conv = wc_ref[CONV_W - 1:CONV_W, cols] * xc
        for k in range(CONV_W - 1):
            shift = CONV_W - 1 - k
            conv = conv + (wc_ref[k:k + 1, cols]
                           * xc_ref[XC_HEAD - shift:XC_HEAD - shift + tm, :])
        y = proj(0) * conv * _silu(proj(3))
        y_ref[:, cols] = y.astype(BF16)
    if final_norm:
        for r0 in range(0, tm, tm // 2):
            rows = slice(r0, r0 + tm // 2)
            out = x[rows] + jnp.dot(y_ref[rows, :], wout_ref[...], preferred_element_type=F32)
            o_ref[rows, :] = out * _rms_scale(out) * fg_ref[...]
    else:
        out = x + jnp.dot(y_ref[...], wout_ref[...], preferred_element_type=F32)
        r_out[...] = _rms_scale(out)
        o_ref[...] = out


def _resident(a, layer=None):
    if layer is None:
        return pl.BlockSpec(a.shape, lambda i: (0,) * a.ndim, pipeline_mode=pl.Buffered(1))
    return pl.BlockSpec((None,) + a.shape[1:], lambda i: (layer,) + (0,) * (a.ndim - 1),
                        pipeline_mode=pl.Buffered(1))


def _slab_rows(w, n_steps):
    rows = w.shape[-2] // n_steps
    assert rows * n_steps == w.shape[-2] and rows % BF16_ROWS == 0
    return rows


def _call_layer(kern, x2, scale, args, in_specs, next_weights, scratch_shapes, *, tm, name):
    n_tok, d_model = x2.shape
    n_steps = n_tok // tm
    row_tile = pl.BlockSpec((tm, d_model), lambda i: (i, 0))
    row_scale = pl.BlockSpec((tm, 1), lambda i: (i, 0))
    slabs_in = [pl.BlockSpec((None, _slab_rows(w, n_steps), w.shape[2]),
                             lambda i, layer=layer: (layer, i, 0)) for w, layer in next_weights]
    slabs_out = [pl.BlockSpec((_slab_rows(w, n_steps), w.shape[2]), lambda i: (i, 0))
                 for w, _ in next_weights]
    has_scale = scale is not None
    emit_scale = bool(next_weights)
    outs = pl.pallas_call(
        functools.partial(kern, tm=tm, n_next=len(next_weights), has_scale=has_scale),
        grid=(n_steps,),
        in_specs=[row_tile] + [row_scale] * has_scale + in_specs + slabs_in,
        out_specs=[row_tile] + [row_scale] * emit_scale + slabs_out,
        out_shape=[jax.ShapeDtypeStruct(x2.shape, x2.dtype)]
                  + [jax.ShapeDtypeStruct((n_tok, 1), F32)] * emit_scale
                  + [jax.ShapeDtypeStruct(w.shape[1:], BF16) for w, _ in next_weights],
        scratch_shapes=scratch_shapes,
        compiler_params=pltpu.CompilerParams(dimension_semantics=("arbitrary",),
                                             vmem_limit_bytes=VMEM_LIMIT_BYTES),
        name=name,
    )(x2, *([scale] * has_scale), *args, *[w for w, _ in next_weights])
    if emit_scale:
        return outs[0], outs[1], outs[2:]
    return outs[0], None, outs[1:]


def _layer_a(x2, scale, norm_g, v_g, w_s, b_s, w_in, w_out, next_weights, *,
             tm, layer, sub, own_weights):
    d_model = x2.shape[1]
    d_inner = w_out.shape[-2]
    kern = functools.partial(_layer_a_kernel, d_inner=d_inner, groups=w_s.shape[1],
                             layer=layer, sub=sub, own_weights=own_weights)
    scratch = [pltpu.VMEM((tm, d_model), BF16),
               pltpu.VMEM((tm, d_inner), BF16),
               pltpu.VMEM((tm, d_inner), BF16),
               pltpu.VMEM(w_s.shape[1:], BF16),
               pltpu.VMEM((CHUNK, d_inner), F32)]
    if own_weights:
        w_specs = [pl.BlockSpec(memory_space=pl.ANY)] * 2
        scratch += [pltpu.VMEM(w_in.shape[1:], BF16),
                    pltpu.VMEM(w_out.shape[1:], BF16),
                    pltpu.VMEM(STAGE_SHAPE, F32),
                    pltpu.SemaphoreType.DMA(STAGE_SHAPE[:1])]
    else:
        w_specs = [_resident(w_in), _resident(w_out)]
    return _call_layer(
        kern, x2, scale,
        [norm_g, v_g, w_s, b_s, w_in, w_out],
        [_resident(norm_g), _resident(v_g), _resident(w_s, sub), _resident(b_s, sub)] + w_specs,
        next_weights, scratch, tm=tm, name="gmlp_layer")


def _layer_b(x2, scale, norm_g, w_conv, final_g, w_in, w_out, next_weights, *,
             tm, seq, col_block, layer, sub, final_norm):
    d_model = x2.shape[1]
    d_inner = w_out.shape[0]
    kern = functools.partial(_layer_b_kernel, d_inner=d_inner, col_block=col_block,
                             tiles_per_seq=seq // tm, layer=layer, final_norm=final_norm)
    return _call_layer(
        kern, x2, scale,
        [norm_g, w_conv, final_g, w_in, w_out],
        [_resident(norm_g), _resident(w_conv, sub), _resident(final_g),
         _resident(w_in), _resident(w_out)],
        next_weights,
        [pltpu.VMEM((tm, d_model), BF16),
         pltpu.VMEM((tm + XC_HEAD, col_block), F32),
         pltpu.VMEM((SUBLANES, d_inner), F32),
         pltpu.VMEM((tm, d_inner), BF16)],
        tm=tm, name="shortconv_layer")


def kernel(x, norm_g, final_g, a_w_in, a_v_norm_g, a_w_s, a_b_s, a_w_out, b_w_in, b_w_conv, b_w_out):
    batch, seq, d_model = x.shape
    depth = norm_g.shape[0]
    tm = 1024
    assert seq % tm == 0 and tm % CHUNK == 0
    assert depth % 2 == 0, "the final norm is fused into the last short-conv layer"
    x2 = x.reshape(batch * seq, d_model)
    assert a_w_s.shape[2:] == (CHUNK, CHUNK) and a_b_s.shape[2] == CHUNK
    final_g = final_g.reshape(1, d_model)
    w_in, w_out, scale = a_w_in, a_w_out, None
    for i in range(depth):
        j = i // 2
        if i % 2 == 0:
            x2, scale, (w_in, w_out) = _layer_a(
                x2, scale, norm_g, a_v_norm_g, a_w_s, a_b_s, w_in, w_out,
                [(b_w_in, j), (b_w_out, j)], tm=tm, layer=i, sub=j, own_weights=i == 0)
        else:
            last = i == depth - 1
            x2, scale, nxt = _layer_b(
                x2, scale, norm_g, b_w_conv, final_g, w_in, w_out,
                [] if last else [(a_w_in, j + 1), (a_w_out, j + 1)],
                tm=tm, seq=seq, col_block=256, layer=i, sub=j, final_norm=last)
            if not last:
                w_in, w_out = nxt
    return x2.reshape(batch, seq, d_model)
```

```python
import functools

import jax
import jax.numpy as jnp
from jax import lax
from jax.experimental import pallas as pl
from jax.experimental.pallas import tpu as pltpu

EPS = 1e-6
CHUNK = 128
CONV_W = 3
SUBLANES = 8
BF16_ROWS = 16
XC_HEAD = 16
VMEM_LIMIT_BYTES = 62 * 1024 * 1024
STAGE_SHAPE = (6, 256, 1024)

F32 = jnp.float32
BF16 = jnp.bfloat16


def _rms_scale(x):
    return lax.rsqrt(jnp.mean(x * x, axis=-1, keepdims=True) + EPS)


def _silu(z):
    return z * (0.5 * jnp.tanh(0.5 * z) + 0.5)


def _round_next_weights(srcs, dsts):
    for src, dst in zip(srcs, dsts):
        dst[...] = src[...].astype(BF16)


def _fetch_and_round(w_hbm, layer, dst_ref, stage_ref, sem):
    max_slots, rows, cols = stage_ref.shape
    k, n = dst_ref.shape
    per_band = n // cols
    n_panels = (k // rows) * per_band
    assert n == per_band * cols and k % rows == 0
    n_slots = max(d for d in range(per_band, max_slots + 1, per_band) if n_panels % d == 0)
    bands_per_group = n_slots // per_band
    n_groups = n_panels // n_slots

    def copy(g, s):
        band, col = divmod(s, per_band)
        r0 = (g * bands_per_group + band) * rows
        c0 = col * cols
        dma = pltpu.make_async_copy(w_hbm.at[layer, pl.ds(r0, rows), pl.ds(c0, cols)],
                                    stage_ref.at[s], sem.at[s])
        return dma, r0, c0

    for s in range(n_slots):
        copy(0, s)[0].start()

    def body(g, carry):
        for s in range(n_slots):
            dma, r0, c0 = copy(g, s)
            dma.wait()
            dst_ref[pl.ds(pl.multiple_of(r0, rows), rows), c0:c0 + cols] = (
                stage_ref[s].astype(BF16))

            @pl.when(g + 1 < n_groups)
            def _():
                copy(g + 1, s)[0].start()
        return carry

    lax.fori_loop(0, n_groups, body, 0)


def _split_refs(rest, n_next, n_scratch):
    n_out = 1 + (1 if n_next else 0)
    assert len(rest) == 2 * n_next + n_out + n_scratch
    r_out = rest[n_next + 1] if n_next else None
    return (rest[:n_next], rest[n_next], r_out, rest[n_next + n_out:2 * n_next + n_out],
            rest[2 * n_next + n_out:])


def _prepare_mixing(ws_in, bs_in, ws_ref, bias_ref):
    groups = ws_in.shape[0]
    gw = bias_ref.shape[1] // groups
    t = lax.broadcasted_iota(jnp.int32, (CHUNK, CHUNK), 0)
    s = lax.broadcasted_iota(jnp.int32, (CHUNK, CHUNK), 1)
    zeros = jnp.zeros((CHUNK, CHUNK), BF16)
    for g in range(groups):
        w = jnp.where(t >= s, ws_in[g], 0.0).astype(BF16)
        ws_ref[g, 0:CHUNK, 0:CHUNK] = w
        ws_ref[g, 0:CHUNK, CHUNK:2 * CHUNK] = zeros
        ws_ref[g, CHUNK:2 * CHUNK, 0:CHUNK] = zeros
        ws_ref[g, CHUNK:2 * CHUNK, CHUNK:2 * CHUNK] = w
        b_col = jnp.sum(jnp.where(t == s, bs_in[g:g + 1, :], 0.0), axis=-1, keepdims=True)
        for half in range(2):
            bias_ref[half * CHUNK:(half + 1) * CHUNK, g * gw:(g + 1) * gw] = (
                jnp.broadcast_to(b_col, (CHUNK, gw)))


def _layer_a_kernel(x_ref, *refs, tm, d_inner, groups, layer, sub, n_next, own_weights, has_scale):
    r_ref, refs = (refs[0], refs[1:]) if has_scale else (None, refs)
    (g_ref, vg_ref, ws_in, bs_in, win_in, wout_in), rest = refs[:6], refs[6:]
    nxt_src, o_ref, r_out, nxt_dst, scratch = _split_refs(rest, n_next, 9 if own_weights else 5)
    h_ref, vn_ref, y_ref, ws_ref, bias_ref = scratch[:5]
    if own_weights:
        win_ref, wout_ref, stage_ref, sem = scratch[5:]
    else:
        win_ref, wout_ref = win_in, wout_in

    @pl.when(pl.program_id(0) == 0)
    def _():
        _prepare_mixing(ws_in, bs_in, ws_ref, bias_ref)
        if own_weights:
            _fetch_and_round(win_in, sub, win_ref, stage_ref, sem)
            _fetch_and_round(wout_in, sub, wout_ref, stage_ref, sem)

    _round_next_weights(nxt_src, nxt_dst)

    x = x_ref[...]
    r = r_ref[...] if has_scale else _rms_scale(x)
    h_ref[...] = (x * r * g_ref[layer:layer + 1, :]).astype(BF16)
    h = h_ref[...]
    v = jnp.dot(h, win_ref[:, d_inner:2 * d_inner], preferred_element_type=F32)
    vn_ref[...] = (v * _rms_scale(v) * vg_ref[sub:sub + 1, :]).astype(BF16)
    gw = d_inner // groups
    for g in range(groups):
        lo = g * gw
        u = jnp.dot(h, win_ref[:, lo:lo + gw], preferred_element_type=F32)
        z = jnp.dot(h, win_ref[:, 2 * d_inner + lo:2 * d_inner + lo + gw],
                    preferred_element_type=F32)
        ws = ws_ref[g]
        for n in range(tm // (2 * CHUNK)):
            r0 = n * 2 * CHUNK
            mixed = jnp.dot(ws, vn_ref[r0:r0 + 2 * CHUNK, lo:lo + gw],
                            preferred_element_type=F32) + bias_ref[:, lo:lo + gw]
            y = u[r0:r0 + 2 * CHUNK] * mixed * _silu(z[r0:r0 + 2 * CHUNK])
            y_ref[r0:r0 + 2 * CHUNK, lo:lo + gw] = y.astype(BF16)
    out = x + jnp.dot(y_ref[...], wout_ref[...], preferred_element_type=F32)
    if r_out is not None:
        r_out[...] = _rms_scale(out)
    o_ref[...] = out


def _layer_b_kernel(x_ref, *refs, tm, d_inner, col_block, tiles_per_seq, layer, final_norm, n_next,
                    has_scale):
    r_ref, refs = (refs[0], refs[1:]) if has_scale else (None, refs)
    (g_ref, wc_ref, fg_ref, win_ref, wout_ref), rest = refs[:5], refs[5:]
    nxt_src, o_ref, r_out, nxt_dst, (h_ref, xc_ref, carry_ref, y_ref) = _split_refs(rest, n_next, 4)

    @pl.when(pl.program_id(0) % tiles_per_seq == 0)
    def _():
        carry_ref[...] = jnp.zeros_like(carry_ref)

    _round_next_weights(nxt_src, nxt_dst)
    x = x_ref[...]
    r = r_ref[...] if has_scale else _rms_scale(x)
    h_ref[...] = (x * r * g_ref[layer:layer + 1, :]).astype(BF16)
    h = h_ref[...]
    for j in range(d_inner // col_block):
        cols = slice(j * col_block, (j + 1) * col_block)

        def proj(k, lo=j * col_block):
            return jnp.dot(h, win_ref[:, k * d_inner + lo:k * d_inner + lo + col_block],
                           preferred_element_type=F32)

        xc = proj(1) * proj(2)
        xc_ref[XC_HEAD - SUBLANES:XC_HEAD, :] = carry_ref[:, cols]
        xc_ref[XC_HEAD:XC_HEAD + tm, :] = xc
        carry_ref[:, cols] = xc[tm - SUBLANES:tm]
        conv = wc_ref[CONV_W - 1:CONV_W, cols] * xc
        for k in range(CONV_W - 1):
            shift = CONV_W - 1 - k
            conv = conv + (wc_ref[k:k + 1, cols]
                           * xc_ref[XC_HEAD - shift:XC_HEAD - shift + tm, :])
        y = proj(0) * conv * _silu(proj(3))
        y_ref[:, cols] = y.astype(BF16)
    out = x + jnp.dot(y_ref[...], wout_ref[...], preferred_element_type=F32)
    if r_out is not None:
        r_out[...] = _rms_scale(out)
    if final_norm:
        out = out * _rms_scale(out) * fg_ref[...]
    o_ref[...] = out


def _resident(a, layer=None):
    if layer is None:
        return pl.BlockSpec(a.shape, lambda i: (0,) * a.ndim, pipeline_mode=pl.Buffered(1))
    return pl.BlockSpec((None,) + a.shape[1:], lambda i: (layer,) + (0,) * (a.ndim - 1),
                        pipeline_mode=pl.Buffered(1))


def _slab_rows(w, n_steps):
    rows = w.shape[-2] // n_steps
    assert rows * n_steps == w.shape[-2] and rows % BF16_ROWS == 0
    return rows


def _call_layer(kern, x2, scale, args, in_specs, next_weights, scratch_shapes, *, tm, name):
    n_tok, d_model = x2.shape
    n_steps = n_tok // tm
    row_tile = pl.BlockSpec((tm, d_model), lambda i: (i, 0))
    row_scale = pl.BlockSpec((tm, 1), lambda i: (i, 0))
    slabs_in = [pl.BlockSpec((None, _slab_rows(w, n_steps), w.shape[2]),
                             lambda i, layer=layer: (layer, i, 0)) for w, layer in next_weights]
    slabs_out = [pl.BlockSpec((_slab_rows(w, n_steps), w.shape[2]), lambda i: (i, 0))
                 for w, _ in next_weights]
    has_scale = scale is not None
    emit_scale = bool(next_weights)
    outs = pl.pallas_call(
        functools.partial(kern, tm=tm, n_next=len(next_weights), has_scale=has_scale),
        grid=(n_steps,),
        in_specs=[row_tile] + [row_scale] * has_scale + in_specs + slabs_in,
        out_specs=[row_tile] + [row_scale] * emit_scale + slabs_out,
        out_shape=[jax.ShapeDtypeStruct(x2.shape, x2.dtype)]
                  + [jax.ShapeDtypeStruct((n_tok, 1), F32)] * emit_scale
                  + [jax.ShapeDtypeStruct(w.shape[1:], BF16) for w, _ in next_weights],
        scratch_shapes=scratch_shapes,
        compiler_params=pltpu.CompilerParams(dimension_semantics=("arbitrary",),
                                             vmem_limit_bytes=VMEM_LIMIT_BYTES),
        name=name,
    )(x2, *([scale] * has_scale), *args, *[w for w, _ in next_weights])
    if emit_scale:
        return outs[0], outs[1], outs[2:]
    return outs[0], None, outs[1:]


def _layer_a(x2, scale, norm_g, v_g, w_s, b_s, w_in, w_out, next_weights, *,
             tm, layer, sub, own_weights):
    d_model = x2.shape[1]
    d_inner = w_out.shape[-2]
    kern = functools.partial(_layer_a_kernel, d_inner=d_inner, groups=w_s.shape[1],
                             layer=layer, sub=sub, own_weights=own_weights)
    scratch = [pltpu.VMEM((tm, d_model), BF16),
               pltpu.VMEM((tm, d_inner), BF16),
               pltpu.VMEM((tm, d_inner), BF16),
               pltpu.VMEM((w_s.shape[1], 2 * CHUNK, 2 * CHUNK), BF16),
               pltpu.VMEM((2 * CHUNK, d_inner), F32)]
    if own_weights:
        w_specs = [pl.BlockSpec(memory_space=pl.ANY)] * 2
        scratch += [pltpu.VMEM(w_in.shape[1:], BF16),
                    pltpu.VMEM(w_out.shape[1:], BF16),
                    pltpu.VMEM(STAGE_SHAPE, F32),
                    pltpu.SemaphoreType.DMA(STAGE_SHAPE[:1])]
    else:
        w_specs = [_resident(w_in), _resident(w_out)]
    return _call_layer(
        kern, x2, scale,
        [norm_g, v_g, w_s, b_s, w_in, w_out],
        [_resident(norm_g), _resident(v_g), _resident(w_s, sub), _resident(b_s, sub)] + w_specs,
        next_weights, scratch, tm=tm, name="gmlp_layer")


def _layer_b(x2, scale, norm_g, w_conv, final_g, w_in, w_out, next_weights, *,
             tm, seq, col_block, layer, sub, final_norm):
    d_model = x2.shape[1]
    d_inner = w_out.shape[0]
    kern = functools.partial(_layer_b_kernel, d_inner=d_inner, col_block=col_block,
                             tiles_per_seq=seq // tm, layer=layer, final_norm=final_norm)
    return _call_layer(
        kern, x2, scale,
        [norm_g, w_conv, final_g, w_in, w_out],
        [_resident(norm_g), _resident(w_conv, sub), _resident(final_g),
         _resident(w_in), _resident(w_out)],
        next_weights,
        [pltpu.VMEM((tm, d_model), BF16),
         pltpu.VMEM((tm + XC_HEAD, col_block), F32),
         pltpu.VMEM((SUBLANES, d_inner), F32),
         pltpu.VMEM((tm, d_inner), BF16)],
        tm=tm, name="shortconv_layer")


def kernel(x, norm_g, final_g, a_w_in, a_v_norm_g, a_w_s, a_b_s, a_w_out, b_w_in, b_w_conv, b_w_out):
    batch, seq, d_model = x.shape
    depth = norm_g.shape[0]
    tm = 1024
    assert seq % tm == 0 and tm % CHUNK == 0
    assert depth % 2 == 0, "the final norm is fused into the last short-conv layer"
    x2 = x.reshape(batch * seq, d_model)
    assert a_w_s.shape[2:] == (CHUNK, CHUNK) and a_b_s.shape[2] == CHUNK
    final_g = final_g.reshape(1, d_model)
    w_in, w_out, scale = a_w_in, a_w_out, None
    for i in range(depth):
        j = i // 2
        if i % 2 == 0:
            x2, scale, (w_in, w_out) = _layer_a(
                x2, scale, norm_g, a_v_norm_g, a_w_s, a_b_s, w_in, w_out,
                [(b_w_in, j), (b_w_out, j)], tm=tm, layer=i, sub=j, own_weights=i == 0)
        else:
            last = i == depth - 1
            x2, scale, nxt = _layer_b(
                x2, scale, norm_g, b_w_conv, final_g, w_in, w_out,
                [] if last else [(a_w_in, j + 1), (a_w_out, j + 1)],
                tm=tm, seq=seq, col_block=256, layer=i, sub=j, final_norm=last)
            if not last:
                w_in, w_out = nxt
    return x2.reshape(batch, seq, d_model)
```

```python
import functools

import jax
import jax.numpy as jnp
from jax import lax
from jax.experimental import pallas as pl
from jax.experimental.pallas import tpu as pltpu

EPS = 1e-6
CHUNK = 128
CONV_W = 3
SUBLANES = 8
BF16_ROWS = 16
XC_HEAD = 16
VMEM_LIMIT_BYTES = 60 * 1024 * 1024
STAGE_SHAPE = (6, 256, 1024)

F32 = jnp.float32
BF16 = jnp.bfloat16


def _rms_scale(x):
    return lax.rsqrt(jnp.mean(x * x, axis=-1, keepdims=True) + EPS)


def _silu(z):
    return z * (0.5 * jnp.tanh(0.5 * z) + 0.5)


def _round_next_weights(srcs, dsts):
    for src, dst in zip(srcs, dsts):
        dst[...] = src[...].astype(BF16)


def _fetch_and_round(w_hbm, layer, dst_ref, stage_ref, sem):
    max_slots, rows, cols = stage_ref.shape
    k, n = dst_ref.shape
    per_band = n // cols
    n_panels = (k // rows) * per_band
    assert n == per_band * cols and k % rows == 0
    n_slots = max(d for d in range(per_band, max_slots + 1, per_band) if n_panels % d == 0)
    bands_per_group = n_slots // per_band
    n_groups = n_panels // n_slots

    def copy(g, s):
        band, col = divmod(s, per_band)
        r0 = (g * bands_per_group + band) * rows
        c0 = col * cols
        dma = pltpu.make_async_copy(w_hbm.at[layer, pl.ds(r0, rows), pl.ds(c0, cols)],
                                    stage_ref.at[s], sem.at[s])
        return dma, r0, c0

    for s in range(n_slots):
        copy(0, s)[0].start()

    def body(g, carry):
        for s in range(n_slots):
            dma, r0, c0 = copy(g, s)
            dma.wait()
            dst_ref[pl.ds(pl.multiple_of(r0, rows), rows), c0:c0 + cols] = (
                stage_ref[s].astype(BF16))

            @pl.when(g + 1 < n_groups)
            def _():
                copy(g + 1, s)[0].start()
        return carry

    lax.fori_loop(0, n_groups, body, 0)


def _split_refs(rest, n_next, n_scratch):
    n_out = 1 + (1 if n_next else 0)
    assert len(rest) == 2 * n_next + n_out + n_scratch
    r_out = rest[n_next + 1] if n_next else None
    return (rest[:n_next], rest[n_next], r_out, rest[n_next + n_out:2 * n_next + n_out],
            rest[2 * n_next + n_out:])


def _prepare_mixing(ws_in, bs_in, ws_ref, bias_ref):
    groups = ws_in.shape[0]
    gw = bias_ref.shape[1] // groups
    t = lax.broadcasted_iota(jnp.int32, (CHUNK, CHUNK), 0)
    s = lax.broadcasted_iota(jnp.int32, (CHUNK, CHUNK), 1)
    for g in range(groups):
        ws_ref[g] = jnp.where(t >= s, ws_in[g], 0.0).astype(BF16)
        b_col = jnp.sum(jnp.where(t == s, bs_in[g:g + 1, :], 0.0), axis=-1, keepdims=True)
        bias_ref[:, g * gw:(g + 1) * gw] = jnp.broadcast_to(b_col, (CHUNK, gw))


def _layer_a_kernel(x_ref, *refs, tm, d_inner, groups, layer, sub, n_next, own_weights, has_scale):
    r_ref, refs = (refs[0], refs[1:]) if has_scale else (None, refs)
    (g_ref, vg_ref, ws_in, bs_in, win_in, wout_in), rest = refs[:6], refs[6:]
    nxt_src, o_ref, r_out, nxt_dst, scratch = _split_refs(rest, n_next, 9 if own_weights else 5)
    h_ref, vn_ref, y_ref, ws_ref, bias_ref = scratch[:5]
    if own_weights:
        win_ref, wout_ref, stage_ref, sem = scratch[5:]
    else:
        win_ref, wout_ref = win_in, wout_in

    @pl.when(pl.program_id(0) == 0)
    def _():
        _prepare_mixing(ws_in, bs_in, ws_ref, bias_ref)
        if own_weights:
            _fetch_and_round(win_in, sub, win_ref, stage_ref, sem)
            _fetch_and_round(wout_in, sub, wout_ref, stage_ref, sem)

    _round_next_weights(nxt_src, nxt_dst)

    x = x_ref[...]
    if has_scale:
        r = jnp.concatenate([r_ref[...]] * (x.shape[1] // 128), axis=1)
    else:
        r = _rms_scale(x)
    h_ref[...] = (x * r * g_ref[layer:layer + 1, :]).astype(BF16)
    h = h_ref[...]
    v = jnp.dot(h, win_ref[:, d_inner:2 * d_inner], preferred_element_type=F32)
    vn_ref[...] = (v * _rms_scale(v) * vg_ref[sub:sub + 1, :]).astype(BF16)
    gw = d_inner // groups
    for g in range(groups):
        lo = g * gw
        u = jnp.dot(h, win_ref[:, lo:lo + gw], preferred_element_type=F32)
        z = jnp.dot(h, win_ref[:, 2 * d_inner + lo:2 * d_inner + lo + gw],
                    preferred_element_type=F32)
        ws = ws_ref[g]
        for n in range(tm // CHUNK):
            r0 = n * CHUNK
            mixed = jnp.dot(ws, vn_ref[r0:r0 + CHUNK, lo:lo + gw],
                            preferred_element_type=F32) + bias_ref[:, lo:lo + gw]
            y = u[r0:r0 + CHUNK] * mixed * _silu(z[r0:r0 + CHUNK])
            y_ref[r0:r0 + CHUNK, lo:lo + gw] = y.astype(BF16)
    out = x + jnp.dot(y_ref[...], wout_ref[...], preferred_element_type=F32)
    if r_out is not None:
        r_out[...] = jnp.broadcast_to(_rms_scale(out), r_out.shape)
    o_ref[...] = out


def _layer_b_kernel(x_ref, *refs, tm, d_inner, col_block, tiles_per_seq, layer, final_norm, n_next,
                    has_scale):
    r_ref, refs = (refs[0], refs[1:]) if has_scale else (None, refs)
    (g_ref, wc_ref, fg_ref, win_ref, wout_ref), rest = refs[:5], refs[5:]
    nxt_src, o_ref, r_out, nxt_dst, (h_ref, xc_ref, carry_ref, y_ref) = _split_refs(rest, n_next, 4)

    @pl.when(pl.program_id(0) % tiles_per_seq == 0)
    def _():
        carry_ref[...] = jnp.zeros_like(carry_ref)

    _round_next_weights(nxt_src, nxt_dst)
    x = x_ref[...]
    if has_scale:
        r = jnp.concatenate([r_ref[...]] * (x.shape[1] // 128), axis=1)
    else:
        r = _rms_scale(x)
    h_ref[...] = (x * r * g_ref[layer:layer + 1, :]).astype(BF16)
    h = h_ref[...]
    for j in range(d_inner // col_block):
        cols = slice(j * col_block, (j + 1) * col_block)

        def proj(k, lo=j * col_block):
            return jnp.dot(h, win_ref[:, k * d_inner + lo:k * d_inner + lo + col_block],
                           preferred_element_type=F32)

        xc = proj(1) * proj(2)
        xc_ref[XC_HEAD - SUBLANES:XC_HEAD, :] = carry_ref[:, cols]
        xc_ref[XC_HEAD:XC_HEAD + tm, :] = xc
        carry_ref[:, cols] = xc[tm - SUBLANES:tm]
        conv = wc_ref[CONV_W - 1:CONV_W, cols] * xc
        for k in range(CONV_W - 1):
            shift = CONV_W - 1 - k
            conv = conv + (wc_ref[k:k + 1, cols]
                           * xc_ref[XC_HEAD - shift:XC_HEAD - shift + tm, :])
        y = proj(0) * conv * _silu(proj(3))
        y_ref[:, cols] = y.astype(BF16)
    out = x + jnp.dot(y_ref[...], wout_ref[...], preferred_element_type=F32)
    if r_out is not None:
        r_out[...] = jnp.broadcast_to(_rms_scale(out), r_out.shape)
    if final_norm:
        out = out * _rms_scale(out) * fg_ref[...]
    o_ref[...] = out


def _resident(a, layer=None):
    if layer is None:
        return pl.BlockSpec(a.shape, lambda i: (0,) * a.ndim, pipeline_mode=pl.Buffered(1))
    return pl.BlockSpec((None,) + a.shape[1:], lambda i: (layer,) + (0,) * (a.ndim - 1),
                        pipeline_mode=pl.Buffered(1))


def _slab_rows(w, n_steps):
    rows = w.shape[-2] // n_steps
    assert rows * n_steps == w.shape[-2] and rows % BF16_ROWS == 0
    return rows


def _call_layer(kern, x2, scale, args, in_specs, next_weights, scratch_shapes, *, tm, name):
    n_tok, d_model = x2.shape
    n_steps = n_tok // tm
    row_tile = pl.BlockSpec((tm, d_model), lambda i: (i, 0))
    row_scale = pl.BlockSpec((tm, 128), lambda i: (i, 0))
    slabs_in = [pl.BlockSpec((None, _slab_rows(w, n_steps), w.shape[2]),
                             lambda i, layer=layer: (layer, i, 0)) for w, layer in next_weights]
    slabs_out = [pl.BlockSpec((_slab_rows(w, n_steps), w.shape[2]), lambda i: (i, 0))
                 for w, _ in next_weights]
    has_scale = scale is not None
    emit_scale = bool(next_weights)
    outs = pl.pallas_call(
        functools.partial(kern, tm=tm, n_next=len(next_weights), has_scale=has_scale),
        grid=(n_steps,),
        in_specs=[row_tile] + [row_scale] * has_scale + in_specs + slabs_in,
        out_specs=[row_tile] + [row_scale] * emit_scale + slabs_out,
        out_shape=[jax.ShapeDtypeStruct(x2.shape, x2.dtype)]
                  + [jax.ShapeDtypeStruct((n_tok, 128), F32)] * emit_scale
                  + [jax.ShapeDtypeStruct(w.shape[1:], BF16) for w, _ in next_weights],
        scratch_shapes=scratch_shapes,
        compiler_params=pltpu.CompilerParams(dimension_semantics=("arbitrary",),
                                             vmem_limit_bytes=VMEM_LIMIT_BYTES),
        name=name,
    )(x2, *([scale] * has_scale), *args, *[w for w, _ in next_weights])
    if emit_scale:
        return outs[0], outs[1], outs[2:]
    return outs[0], None, outs[1:]


def _layer_a(x2, scale, norm_g, v_g, w_s, b_s, w_in, w_out, next_weights, *,
             tm, layer, sub, own_weights):
    d_model = x2.shape[1]
    d_inner = w_out.shape[-2]
    kern = functools.partial(_layer_a_kernel, d_inner=d_inner, groups=w_s.shape[1],
                             layer=layer, sub=sub, own_weights=own_weights)
    scratch = [pltpu.VMEM((tm, d_model), BF16),
               pltpu.VMEM((tm, d_inner), BF16),
               pltpu.VMEM((tm, d_inner), BF16),
               pltpu.VMEM(w_s.shape[1:], BF16),
               pltpu.VMEM((CHUNK, d_inner), F32)]
    if own_weights:
        w_specs = [pl.BlockSpec(memory_space=pl.ANY)] * 2
        scratch += [pltpu.VMEM(w_in.shape[1:], BF16),
                    pltpu.VMEM(w_out.shape[1:], BF16),
                    pltpu.VMEM(STAGE_SHAPE, F32),
                    pltpu.SemaphoreType.DMA(STAGE_SHAPE[:1])]
    else:
        w_specs = [_resident(w_in), _resident(w_out)]
    return _call_layer(
        kern, x2, scale,
        [norm_g, v_g, w_s, b_s, w_in, w_out],
        [_resident(norm_g), _resident(v_g), _resident(w_s, sub), _resident(b_s, sub)] + w_specs,
        next_weights, scratch, tm=tm, name="gmlp_layer")


def _layer_b(x2, scale, norm_g, w_conv, final_g, w_in, w_out, next_weights, *,
             tm, seq, col_block, layer, sub, final_norm):
    d_model = x2.shape[1]
    d_inner = w_out.shape[0]
    kern = functools.partial(_layer_b_kernel, d_inner=d_inner, col_block=col_block,
                             tiles_per_seq=seq // tm, layer=layer, final_norm=final_norm)
    return _call_layer(
        kern, x2, scale,
        [norm_g, w_conv, final_g, w_in, w_out],
        [_resident(norm_g), _resident(w_conv, sub), _resident(final_g),
         _resident(w_in), _resident(w_out)],
        next_weights,
        [pltpu.VMEM((tm, d_model), BF16),
         pltpu.VMEM((tm + XC_HEAD, col_block), F32),
         pltpu.VMEM((SUBLANES, d_inner), F32),
         pltpu.VMEM((tm, d_inner), BF16)],
        tm=tm, name="shortconv_layer")


def kernel(x, norm_g, final_g, a_w_in, a_v_norm_g, a_w_s, a_b_s, a_w_out, b_w_in, b_w_conv, b_w_out):
    batch, seq, d_model = x.shape
    depth = norm_g.shape[0]
    tm = 1024
    assert seq % tm == 0 and tm % CHUNK == 0
    assert depth % 2 == 0, "the final norm is fused into the last short-conv layer"
    x2 = x.reshape(batch * seq, d_model)
    assert a_w_s.shape[2:] == (CHUNK, CHUNK) and a_b_s.shape[2] == CHUNK
    final_g = final_g.reshape(1, d_model)
    w_in, w_out, scale = a_w_in, a_w_out, None
    for i in range(depth):
        j = i // 2
        if i % 2 == 0:
            x2, scale, (w_in, w_out) = _layer_a(
                x2, scale, norm_g, a_v_norm_g, a_w_s, a_b_s, w_in, w_out,
                [(b_w_in, j), (b_w_out, j)], tm=tm, layer=i, sub=j, own_weights=i == 0)
        else:
            last = i == depth - 1
            x2, scale, nxt = _layer_b(
                x2, scale, norm_g, b_w_conv, final_g, w_in, w_out,
                [] if last else [(a_w_in, j + 1), (a_w_out, j + 1)],
                tm=tm, seq=seq, col_block=256, layer=i, sub=j, final_norm=last)
            if not last:
                w_in, w_out = nxt
    return x2.reshape(batch, seq, d_model)
```

```python
import functools

import jax
import jax.numpy as jnp
from jax import lax
from jax.experimental import pallas as pl
from jax.experimental.pallas import tpu as pltpu

EPS = 1e-6
CHUNK = 128
CONV_W = 3
SUBLANES = 8
LANES = 128
BF16_ROWS = 16
XC_HEAD = 16
VMEM_LIMIT_BYTES = 60 * 1024 * 1024
STAGE_SHAPE = (6, 256, 1024)

F32 = jnp.float32
BF16 = jnp.bfloat16


def _rms_scale(x):
    return lax.rsqrt(jnp.mean(x * x, axis=-1, keepdims=True) + EPS)


def _silu(z):
    return z * (0.5 * jnp.tanh(0.5 * z) + 0.5)


def _round_next_weights(srcs, dsts):
    for src, dst in zip(srcs, dsts):
        dst[...] = src[...].astype(BF16)


def _fetch_and_round(w_hbm, layer, dst_ref, stage_ref, sem):
    max_slots, rows, cols = stage_ref.shape
    k, n = dst_ref.shape
    per_band = n // cols
    n_panels = (k // rows) * per_band
    assert n == per_band * cols and k % rows == 0
    n_slots = max(d for d in range(per_band, max_slots + 1, per_band) if n_panels % d == 0)
    bands_per_group = n_slots // per_band
    n_groups = n_panels // n_slots

    def copy(g, s):
        band, col = divmod(s, per_band)
        r0 = (g * bands_per_group + band) * rows
        c0 = col * cols
        dma = pltpu.make_async_copy(w_hbm.at[layer, pl.ds(r0, rows), pl.ds(c0, cols)],
                                    stage_ref.at[s], sem.at[s])
        return dma, r0, c0

    for s in range(n_slots):
        copy(0, s)[0].start()

    def body(g, carry):
        for s in range(n_slots):
            dma, r0, c0 = copy(g, s)
            dma.wait()
            dst_ref[pl.ds(pl.multiple_of(r0, rows), rows), c0:c0 + cols] = (
                stage_ref[s].astype(BF16))

            @pl.when(g + 1 < n_groups)
            def _():
                copy(g + 1, s)[0].start()
        return carry

    lax.fori_loop(0, n_groups, body, 0)


def _split_refs(rest, n_next, n_scratch):
    n_out = 1 + (1 if n_next else 0)
    assert len(rest) == 2 * n_next + n_out + n_scratch
    r_out = rest[n_next + 1] if n_next else None
    return (rest[:n_next], rest[n_next], r_out, rest[n_next + n_out:2 * n_next + n_out],
            rest[2 * n_next + n_out:])


def _prepare_mixing(ws_in, bs_in, ws_ref, bias_ref):
    groups = ws_in.shape[0]
    gw = bias_ref.shape[1] // groups
    t = lax.broadcasted_iota(jnp.int32, (CHUNK, CHUNK), 0)
    s = lax.broadcasted_iota(jnp.int32, (CHUNK, CHUNK), 1)
    for g in range(groups):
        ws_ref[g] = jnp.where(t >= s, ws_in[g], 0.0).astype(BF16)
        b_col = jnp.sum(jnp.where(t == s, bs_in[g:g + 1, :], 0.0), axis=-1, keepdims=True)
        bias_ref[:, g * gw:(g + 1) * gw] = jnp.broadcast_to(b_col, (CHUNK, gw))


def _layer_a_kernel(x_ref, *refs, tm, d_inner, groups, layer, sub, n_next, own_weights, has_scale):
    r_ref, refs = (refs[0], refs[1:]) if has_scale else (None, refs)
    (g_ref, vg_ref, ws_in, bs_in, win_in, wout_in), rest = refs[:6], refs[6:]
    nxt_src, o_ref, r_out, nxt_dst, scratch = _split_refs(rest, n_next, 9 if own_weights else 5)
    h_ref, vn_ref, y_ref, ws_ref, bias_ref = scratch[:5]
    if own_weights:
        win_ref, wout_ref, stage_ref, sem = scratch[5:]
    else:
        win_ref, wout_ref = win_in, wout_in

    @pl.when(pl.program_id(0) == 0)
    def _():
        _prepare_mixing(ws_in, bs_in, ws_ref, bias_ref)
        if own_weights:
            _fetch_and_round(win_in, sub, win_ref, stage_ref, sem)
            _fetch_and_round(wout_in, sub, wout_ref, stage_ref, sem)

    _round_next_weights(nxt_src, nxt_dst)

    x = x_ref[...]
    if has_scale:
        r = jnp.concatenate([r_ref[...]] * (x.shape[1] // LANES), axis=1)
    else:
        r = _rms_scale(x)
    h_ref[...] = (x * r * g_ref[layer:layer + 1, :]).astype(BF16)
    h = h_ref[...]
    v = jnp.dot(h, win_ref[:, d_inner:2 * d_inner], preferred_element_type=F32)
    vn_ref[...] = (v * _rms_scale(v) * vg_ref[sub:sub + 1, :]).astype(BF16)
    gw = d_inner // groups
    for g in range(groups):
        lo = g * gw
        u = jnp.dot(h, win_ref[:, lo:lo + gw], preferred_element_type=F32)
        z = jnp.dot(h, win_ref[:, 2 * d_inner + lo:2 * d_inner + lo + gw],
                    preferred_element_type=F32)
        ws = ws_ref[g]
        for n in range(tm // CHUNK):
            r0 = n * CHUNK
            mixed = jnp.dot(ws, vn_ref[r0:r0 + CHUNK, lo:lo + gw],
                            preferred_element_type=F32) + bias_ref[:, lo:lo + gw]
            y = u[r0:r0 + CHUNK] * mixed * _silu(z[r0:r0 + CHUNK])
            y_ref[r0:r0 + CHUNK, lo:lo + gw] = y.astype(BF16)
    out = x + jnp.dot(y_ref[...], wout_ref[...], preferred_element_type=F32)
    if r_out is not None:
        r_out[...] = jnp.broadcast_to(_rms_scale(out), r_out.shape)
    o_ref[...] = out


def _layer_b_kernel(x_ref, *refs, tm, d_inner, col_block, tiles_per_seq, layer, final_norm, n_next,
                    has_scale):
    r_ref, refs = (refs[0], refs[1:]) if has_scale else (None, refs)
    (g_ref, wc_ref, fg_ref, win_ref, wout_ref), rest = refs[:5], refs[5:]
    nxt_src, o_ref, r_out, nxt_dst, (h_ref, xc_ref, carry_ref, y_ref) = _split_refs(rest, n_next, 4)

    @pl.when(pl.program_id(0) % tiles_per_seq == 0)
    def _():
        carry_ref[...] = jnp.zeros_like(carry_ref)

    _round_next_weights(nxt_src, nxt_dst)
    x = x_ref[...]
    if has_scale:
        r = jnp.concatenate([r_ref[...]] * (x.shape[1] // LANES), axis=1)
    else:
        r = _rms_scale(x)
    h_ref[...] = (x * r * g_ref[layer:layer + 1, :]).astype(BF16)
    h = h_ref[...]
    for j in range(d_inner // col_block):
        cols = slice(j * col_block, (j + 1) * col_block)

        def proj(k, lo=j * col_block):
            return jnp.dot(h, win_ref[:, k * d_inner + lo:k * d_inner + lo + col_block],
                           preferred_element_type=F32)

        xc = proj(1) * proj(2)
        xc_ref[XC_HEAD - SUBLANES:XC_HEAD, :] = carry_ref[:, cols]
        xc_ref[XC_HEAD:XC_HEAD + tm, :] = xc
        carry_ref[:, cols] = xc[tm - SUBLANES:tm]
        conv = wc_ref[CONV_W - 1:CONV_W, cols] * xc
        for k in range(CONV_W - 1):
            shift = CONV_W - 1 - k
            conv = conv + (wc_ref[k:k + 1, cols]
                           * xc_ref[XC_HEAD - shift:XC_HEAD - shift + tm, :])
        y = proj(0) * conv * _silu(proj(3))
        y_ref[:, cols] = y.astype(BF16)
    out = x + jnp.dot(y_ref[...], wout_ref[...], preferred_element_type=F32)
    if r_out is not None:
        r_out[...] = jnp.broadcast_to(_rms_scale(out), r_out.shape)
    if final_norm:
        out = out * _rms_scale(out) * fg_ref[...]
    o_ref[...] = out


def _resident(a, layer=None):
    if layer is None:
        return pl.BlockSpec(a.shape, lambda i: (0,) * a.ndim, pipeline_mode=pl.Buffered(1))
    return pl.BlockSpec((None,) + a.shape[1:], lambda i: (layer,) + (0,) * (a.ndim - 1),
                        pipeline_mode=pl.Buffered(1))


def _slab_rows(w, n_steps):
    rows = w.shape[-2] // n_steps
    assert rows * n_steps == w.shape[-2] and rows % BF16_ROWS == 0
    return rows


def _call_layer(kern, x2, scale, args, in_specs, next_weights, scratch_shapes, *, tm, name):
    n_tok, d_model = x2.shape
    n_steps = n_tok // tm
    row_tile = pl.BlockSpec((tm, d_model), lambda i: (i, 0))
    row_scale = pl.BlockSpec((tm, LANES), lambda i: (i, 0))
    slabs_in = [pl.BlockSpec((None, _slab_rows(w, n_steps), w.shape[2]),
                             lambda i, layer=layer: (layer, i, 0)) for w, layer in next_weights]
    slabs_out = [pl.BlockSpec((_slab_rows(w, n_steps), w.shape[2]), lambda i: (i, 0))
                 for w, _ in next_weights]
    has_scale = scale is not None
    emit_scale = bool(next_weights)
    outs = pl.pallas_call(
        functools.partial(kern, tm=tm, n_next=len(next_weights), has_scale=has_scale),
        grid=(n_steps,),
        in_specs=[row_tile] + [row_scale] * has_scale + in_specs + slabs_in,
        out_specs=[row_tile] + [row_scale] * emit_scale + slabs_out,
        out_shape=[jax.ShapeDtypeStruct(x2.shape, x2.dtype)]
                  + [jax.ShapeDtypeStruct((n_tok, LANES), F32)] * emit_scale
                  + [jax.ShapeDtypeStruct(w.shape[1:], BF16) for w, _ in next_weights],
        scratch_shapes=scratch_shapes,
        compiler_params=pltpu.CompilerParams(dimension_semantics=("arbitrary",),
                                             vmem_limit_bytes=VMEM_LIMIT_BYTES),
        name=name,
    )(x2, *([scale] * has_scale), *args, *[w for w, _ in next_weights])
    if emit_scale:
        return outs[0], outs[1], outs[2:]
    return outs[0], None, outs[1:]


def _layer_a(x2, scale, norm_g, v_g, w_s, b_s, w_in, w_out, next_weights, *,
             tm, layer, sub, own_weights):
    d_model = x2.shape[1]
    d_inner = w_out.shape[-2]
    kern = functools.partial(_layer_a_kernel, d_inner=d_inner, groups=w_s.shape[1],
                             layer=layer, sub=sub, own_weights=own_weights)
    scratch = [pltpu.VMEM((tm, d_model), BF16),
               pltpu.VMEM((tm, d_inner), BF16),
               pltpu.VMEM((tm, d_inner), BF16),
               pltpu.VMEM(w_s.shape[1:], BF16),
               pltpu.VMEM((CHUNK, d_inner), F32)]
    if own_weights:
        w_specs = [pl.BlockSpec(memory_space=pl.ANY)] * 2
        scratch += [pltpu.VMEM(w_in.shape[1:], BF16),
                    pltpu.VMEM(w_out.shape[1:], BF16),
                    pltpu.VMEM(STAGE_SHAPE, F32),
                    pltpu.SemaphoreType.DMA(STAGE_SHAPE[:1])]
    else:
        w_specs = [_resident(w_in), _resident(w_out)]
    return _call_layer(
        kern, x2, scale,
        [norm_g, v_g, w_s, b_s, w_in, w_out],
        [_resident(norm_g), _resident(v_g), _resident(w_s, sub), _resident(b_s, sub)] + w_specs,
        next_weights, scratch, tm=tm, name="gmlp_layer")


def _layer_b(x2, scale, norm_g, w_conv, final_g, w_in, w_out, next_weights, *,
             tm, seq, col_block, layer, sub, final_norm):
    d_model = x2.shape[1]
    d_inner = w_out.shape[0]
    kern = functools.partial(_layer_b_kernel, d_inner=d_inner, col_block=col_block,
                             tiles_per_seq=seq // tm, layer=layer, final_norm=final_norm)
    return _call_layer(
        kern, x2, scale,
        [norm_g, w_conv, final_g, w_in, w_out],
        [_resident(norm_g), _resident(w_conv, sub), _resident(final_g),
         _resident(w_in), _resident(w_out)],
        next_weights,
        [pltpu.VMEM((tm, d_model), BF16),
         pltpu.VMEM((tm + XC_HEAD, col_block), F32),
         pltpu.VMEM((SUBLANES, d_inner), F32),
         pltpu.VMEM((tm, d_inner), BF16)],
        tm=tm, name="shortconv_layer")


def kernel(x, norm_g, final_g, a_w_in, a_v_norm_g, a_w_s, a_b_s, a_w_out, b_w_in, b_w_conv, b_w_out):
    batch, seq, d_model = x.shape
    depth = norm_g.shape[0]
    tm = 1024
    assert seq % tm == 0 and tm % CHUNK == 0
    assert depth % 2 == 0, "the final norm is fused into the last short-conv layer"
    x2 = x.reshape(batch * seq, d_model)
    assert a_w_s.shape[2:] == (CHUNK, CHUNK) and a_b_s.shape[2] == CHUNK
    final_g = final_g.reshape(1, d_model)
    w_in, w_out, scale = a_w_in, a_w_out, None
    for i in range(depth):
        j = i // 2
        if i % 2 == 0:
            x2, scale, (w_in, w_out) = _layer_a(
                x2, scale, norm_g, a_v_norm_g, a_w_s, a_b_s, w_in, w_out,
                [(b_w_in, j), (b_w_out, j)], tm=tm, layer=i, sub=j, own_weights=i == 0)
        else:
            last = i == depth - 1
            x2, scale, nxt = _layer_b(
                x2, scale, norm_g, b_w_conv, final_g, w_in, w_out,
                [] if last else [(a_w_in, j + 1), (a_w_out, j + 1)],
                tm=tm, seq=seq, col_block=256, layer=i, sub=j, final_norm=last)
            if not last:
                w_in, w_out = nxt
    return x2.reshape(batch, seq, d_model)
```

```python
import functools

import jax
import jax.numpy as jnp
from jax import lax
from jax.experimental import pallas as pl
from jax.experimental.pallas import tpu as pltpu

EPS = 1e-6
CHUNK = 128
CONV_W = 3
SUBLANES = 8
LANES = 128
BF16_ROWS = 16
XC_HEAD = 16
VMEM_LIMIT_BYTES = 60 * 1024 * 1024
STAGE_SHAPE = (6, 256, 1024)

F32 = jnp.float32
BF16 = jnp.bfloat16


def _rms_scale(x):
    return lax.rsqrt(jnp.mean(x * x, axis=-1, keepdims=True) + EPS)


def _silu(z):
    return z * (0.5 * jnp.tanh(0.5 * z) + 0.5)


def _round_next_weights(srcs, dsts):
    for src, dst in zip(srcs, dsts):
        dst[...] = src[...].astype(BF16)


def _fetch_and_round(w_hbm, layer, dst_ref, stage_ref, sem):
    max_slots, rows, cols = stage_ref.shape
    k, n = dst_ref.shape
    per_band = n // cols
    n_panels = (k // rows) * per_band
    assert n == per_band * cols and k % rows == 0
    n_slots = max(d for d in range(per_band, max_slots + 1, per_band) if n_panels % d == 0)
    bands_per_group = n_slots // per_band
    n_groups = n_panels // n_slots

    def copy(g, s):
        band, col = divmod(s, per_band)
        r0 = (g * bands_per_group + band) * rows
        c0 = col * cols
        dma = pltpu.make_async_copy(w_hbm.at[layer, pl.ds(r0, rows), pl.ds(c0, cols)],
                                    stage_ref.at[s], sem.at[s])
        return dma, r0, c0

    for s in range(n_slots):
        copy(0, s)[0].start()

    def body(g, carry):
        for s in range(n_slots):
            dma, r0, c0 = copy(g, s)
            dma.wait()
            dst_ref[pl.ds(pl.multiple_of(r0, rows), rows), c0:c0 + cols] = (
                stage_ref[s].astype(BF16))

            @pl.when(g + 1 < n_groups)
            def _():
                copy(g + 1, s)[0].start()
        return carry

    lax.fori_loop(0, n_groups, body, 0)


def _split_refs(rest, n_next, n_scratch):
    n_out = 1 + (1 if n_next else 0)
    assert len(rest) == 2 * n_next + n_out + n_scratch
    r_out = rest[n_next + 1] if n_next else None
    return (rest[:n_next], rest[n_next], r_out, rest[n_next + n_out:2 * n_next + n_out],
            rest[2 * n_next + n_out:])


def _prepare_mixing(ws_in, bs_in, ws_ref, bias_ref):
    groups = ws_in.shape[0]
    gw = bias_ref.shape[1] // groups
    t = lax.broadcasted_iota(jnp.int32, (CHUNK, CHUNK), 0)
    s = lax.broadcasted_iota(jnp.int32, (CHUNK, CHUNK), 1)
    for g in range(groups):
        ws_ref[g] = jnp.where(t >= s, ws_in[g], 0.0).astype(BF16)
        b_col = jnp.sum(jnp.where(t == s, bs_in[g:g + 1, :], 0.0), axis=-1, keepdims=True)
        bias_ref[:, g * gw:(g + 1) * gw] = jnp.broadcast_to(b_col, (CHUNK, gw))


def _layer_a_kernel(x_ref, *refs, tm, d_inner, groups, layer, sub, n_next, own_weights, has_scale):
    r_ref, refs = (refs[0], refs[1:]) if has_scale else (None, refs)
    (g_ref, vg_ref, ws_in, bs_in, win_in, wout_in), rest = refs[:6], refs[6:]
    nxt_src, o_ref, r_out, nxt_dst, scratch = _split_refs(rest, n_next, 9 if own_weights else 5)
    h_ref, vn_ref, y_ref, ws_ref, bias_ref = scratch[:5]
    if own_weights:
        win_ref, wout_ref, stage_ref, sem = scratch[5:]
    else:
        win_ref, wout_ref = win_in, wout_in

    @pl.when(pl.program_id(0) == 0)
    def _():
        _prepare_mixing(ws_in, bs_in, ws_ref, bias_ref)
        if own_weights:
            _fetch_and_round(win_in, sub, win_ref, stage_ref, sem)
            _fetch_and_round(wout_in, sub, wout_ref, stage_ref, sem)

    _round_next_weights(nxt_src, nxt_dst)

    x = x_ref[...]
    if has_scale:
        r = jnp.concatenate([r_ref[...]] * (x.shape[1] // LANES), axis=1)
    else:
        r = _rms_scale(x)
    h_ref[...] = (x * r * g_ref[layer:layer + 1, :]).astype(BF16)
    h = h_ref[...]
    v = jnp.dot(h, win_ref[:, d_inner:2 * d_inner], preferred_element_type=F32)
    vn_ref[...] = (v * _rms_scale(v) * vg_ref[sub:sub + 1, :]).astype(BF16)
    gw = d_inner // groups
    for g in range(groups):
        lo = g * gw
        u = jnp.dot(h, win_ref[:, lo:lo + gw], preferred_element_type=F32)
        z = jnp.dot(h, win_ref[:, 2 * d_inner + lo:2 * d_inner + lo + gw],
                    preferred_element_type=F32)
        ws = ws_ref[g]
        for n in range(tm // CHUNK):
            r0 = n * CHUNK
            mixed = jnp.dot(ws, vn_ref[r0:r0 + CHUNK, lo:lo + gw],
                            preferred_element_type=F32) + bias_ref[:, lo:lo + gw]
            y = u[r0:r0 + CHUNK] * mixed * _silu(z[r0:r0 + CHUNK])
            y_ref[r0:r0 + CHUNK, lo:lo + gw] = y.astype(BF16)
    out = x + jnp.dot(y_ref[...], wout_ref[...], preferred_element_type=F32)
    if r_out is not None:
        r_out[...] = jnp.broadcast_to(_rms_scale(out), r_out.shape)
    o_ref[...] = out


def _layer_b_kernel(x_ref, *refs, tm, d_inner, col_block, tiles_per_seq, layer, final_norm, n_next,
                    has_scale):
    r_ref, refs = (refs[0], refs[1:]) if has_scale else (None, refs)
    (g_ref, wc_ref, fg_ref, win_ref, wout_ref), rest = refs[:5], refs[5:]
    nxt_src, o_ref, r_out, nxt_dst, (h_ref, xc_ref, carry_ref, y_ref) = _split_refs(rest, n_next, 4)

    @pl.when(pl.program_id(0) % tiles_per_seq == 0)
    def _():
        carry_ref[...] = jnp.zeros_like(carry_ref)

    _round_next_weights(nxt_src, nxt_dst)
    x = x_ref[...]
    if has_scale:
        r = jnp.concatenate([r_ref[...]] * (x.shape[1] // LANES), axis=1)
    else:
        r = _rms_scale(x)
    h_ref[...] = (x * r * g_ref[layer:layer + 1, :]).astype(BF16)
    h = h_ref[...]
    for j in range(d_inner // col_block):
        cols = slice(j * col_block, (j + 1) * col_block)

        def proj(k, lo=j * col_block):
            return jnp.dot(h, win_ref[:, k * d_inner + lo:k * d_inner + lo + col_block],
                           preferred_element_type=F32)

        xc = proj(1) * proj(2)
        xc_ref[XC_HEAD - SUBLANES:XC_HEAD, :] = carry_ref[:, cols]
        xc_ref[XC_HEAD:XC_HEAD + tm, :] = xc
        carry_ref[:, cols] = xc[tm - SUBLANES:tm]
        conv = wc_ref[CONV_W - 1:CONV_W, cols] * xc
        for k in range(CONV_W - 1):
            shift = CONV_W - 1 - k
            conv = conv + (wc_ref[k:k + 1, cols]
                           * xc_ref[XC_HEAD - shift:XC_HEAD - shift + tm, :])
        y = proj(0) * conv * _silu(proj(3))
        y_ref[:, cols] = y.astype(BF16)
    out = x + jnp.dot(y_ref[...], wout_ref[...], preferred_element_type=F32)
    if r_out is not None:
        r_out[...] = jnp.broadcast_to(_rms_scale(out), r_out.shape)
    if final_norm:
        out = out * _rms_scale(out) * fg_ref[...]
    o_ref[...] = out


def _resident(a, layer=None):
    if layer is None:
        return pl.BlockSpec(a.shape, lambda i: (0,) * a.ndim, pipeline_mode=pl.Buffered(1))
    return pl.BlockSpec((None,) + a.shape[1:], lambda i: (layer,) + (0,) * (a.ndim - 1),
                        pipeline_mode=pl.Buffered(1))


def _slab_rows(w, n_steps):
    rows = w.shape[-2] // n_steps
    assert rows * n_steps == w.shape[-2] and rows % BF16_ROWS == 0
    return rows


def _call_layer(kern, x2, scale, args, in_specs, next_weights, scratch_shapes, *, tm, name):
    n_tok, d_model = x2.shape
    n_steps = n_tok // tm
    row_tile = pl.BlockSpec((tm, d_model), lambda i: (i, 0))
    row_scale = pl.BlockSpec((tm, LANES), lambda i: (i, 0))
    slabs_in = [pl.BlockSpec((None, _slab_rows(w, n_steps), w.shape[2]),
                             lambda i, layer=layer: (layer, i, 0)) for w, layer in next_weights]
    slabs_out = [pl.BlockSpec((_slab_rows(w, n_steps), w.shape[2]), lambda i: (i, 0))
                 for w, _ in next_weights]
    has_scale = scale is not None
    emit_scale = bool(next_weights)
    outs = pl.pallas_call(
        functools.partial(kern, tm=tm, n_next=len(next_weights), has_scale=has_scale),
        grid=(n_steps,),
        in_specs=[row_tile] + [row_scale] * has_scale + in_specs + slabs_in,
        out_specs=[row_tile] + [row_scale] * emit_scale + slabs_out,
        out_shape=[jax.ShapeDtypeStruct(x2.shape, x2.dtype)]
                  + [jax.ShapeDtypeStruct((n_tok, LANES), F32)] * emit_scale
                  + [jax.ShapeDtypeStruct(w.shape[1:], BF16) for w, _ in next_weights],
        scratch_shapes=scratch_shapes,
        input_output_aliases={0: 0} if has_scale else {},
        compiler_params=pltpu.CompilerParams(dimension_semantics=("arbitrary",),
                                             vmem_limit_bytes=VMEM_LIMIT_BYTES),
        name=name,
    )(x2, *([scale] * has_scale), *args, *[w for w, _ in next_weights])
    if emit_scale:
        return outs[0], outs[1], outs[2:]
    return outs[0], None, outs[1:]


def _layer_a(x2, scale, norm_g, v_g, w_s, b_s, w_in, w_out, next_weights, *,
             tm, layer, sub, own_weights):
    d_model = x2.shape[1]
    d_inner = w_out.shape[-2]
    kern = functools.partial(_layer_a_kernel, d_inner=d_inner, groups=w_s.shape[1],
                             layer=layer, sub=sub, own_weights=own_weights)
    scratch = [pltpu.VMEM((tm, d_model), BF16),
               pltpu.VMEM((tm, d_inner), BF16),
               pltpu.VMEM((tm, d_inner), BF16),
               pltpu.VMEM(w_s.shape[1:], BF16),
               pltpu.VMEM((CHUNK, d_inner), F32)]
    if own_weights:
        w_specs = [pl.BlockSpec(memory_space=pl.ANY)] * 2
        scratch += [pltpu.VMEM(w_in.shape[1:], BF16),
                    pltpu.VMEM(w_out.shape[1:], BF16),
                    pltpu.VMEM(STAGE_SHAPE, F32),
                    pltpu.SemaphoreType.DMA(STAGE_SHAPE[:1])]
    else:
        w_specs = [_resident(w_in), _resident(w_out)]
    return _call_layer(
        kern, x2, scale,
        [norm_g, v_g, w_s, b_s, w_in, w_out],
        [_resident(norm_g), _resident(v_g), _resident(w_s, sub), _resident(b_s, sub)] + w_specs,
        next_weights, scratch, tm=tm, name="gmlp_layer")


def _layer_b(x2, scale, norm_g, w_conv, final_g, w_in, w_out, next_weights, *,
             tm, seq, col_block, layer, sub, final_norm):
    d_model = x2.shape[1]
    d_inner = w_out.shape[0]
    kern = functools.partial(_layer_b_kernel, d_inner=d_inner, col_block=col_block,
                             tiles_per_seq=seq // tm, layer=layer, final_norm=final_norm)
    return _call_layer(
        kern, x2, scale,
        [norm_g, w_conv, final_g, w_in, w_out],
        [_resident(norm_g), _resident(w_conv, sub), _resident(final_g),
         _resident(w_in), _resident(w_out)],
        next_weights,
        [pltpu.VMEM((tm, d_model), BF16),
         pltpu.VMEM((tm + XC_HEAD, col_block), F32),
         pltpu.VMEM((SUBLANES, d_inner), F32),
         pltpu.VMEM((tm, d_inner), BF16)],
        tm=tm, name="shortconv_layer")


def kernel(x, norm_g, final_g, a_w_in, a_v_norm_g, a_w_s, a_b_s, a_w_out, b_w_in, b_w_conv, b_w_out):
    batch, seq, d_model = x.shape
    depth = norm_g.shape[0]
    tm = 1024
    assert seq % tm == 0 and tm % CHUNK == 0
    assert depth % 2 == 0, "the final norm is fused into the last short-conv layer"
    x2 = x.reshape(batch * seq, d_model)
    assert a_w_s.shape[2:] == (CHUNK, CHUNK) and a_b_s.shape[2] == CHUNK
    final_g = final_g.reshape(1, d_model)
    w_in, w_out, scale = a_w_in, a_w_out, None
    for i in range(depth):
        j = i // 2
        if i % 2 == 0:
            x2, scale, (w_in, w_out) = _layer_a(
                x2, scale, norm_g, a_v_norm_g, a_w_s, a_b_s, w_in, w_out,
                [(b_w_in, j), (b_w_out, j)], tm=tm, layer=i, sub=j, own_weights=i == 0)
        else:
            last = i == depth - 1
            x2, scale, nxt = _layer_b(
                x2, scale, norm_g, b_w_conv, final_g, w_in, w_out,
                [] if last else [(a_w_in, j + 1), (a_w_out, j + 1)],
                tm=tm, seq=seq, col_block=256, layer=i, sub=j, final_norm=last)
            if not last:
                w_in, w_out = nxt
    return x2.reshape(batch, seq, d_model)
```
